```python
import jax, jax.numpy as jnp
from jax import lax
import numpy as np

D_MODEL = 1024
BATCH = 4
SEQ = 4096
DEPTH = 4

GRID_W = 64
CTX_LEN = 256
CONV_W = 256
FOUR_GROUPS = 4
FOUR_GROUP_DIM = 64
FOUR_W = FOUR_GROUPS * FOUR_GROUP_DIM
N_HEADS = 8
N_KV = 2
GROUP = N_HEADS // N_KV
HEAD_DIM = 64
ATT_W = N_HEADS * HEAD_DIM
KV_W = N_KV * HEAD_DIM
MIX_W = CONV_W + FOUR_W + ATT_W
Q_START = 3 * CONV_W + FOUR_W
K_START = Q_START + ATT_W
V_START = K_START + KV_W
PROJ_W = V_START + KV_W
AXIS_DIM = HEAD_DIM // 2
ROPE_THETA = 10000.0
Q_BLOCK = 128
D_FF = 4 * D_MODEL
N_MOD = 6
EPS = 1e-6

kernel_name = "hybrid_conv_fourier_gqa_dit_block"


def rms_norm(x, g):
    xf = x.astype(jnp.float32)
    y = xf * lax.rsqrt(jnp.mean(xf * xf, axis=-1, keepdims=True) + EPS)
    return (y * g.astype(jnp.float32)).astype(x.dtype)


def modulate(h, shift, scale):
    return h * (1 + scale) + shift


def rope_tables(row, col, dtype):
    inv = 1.0 / (ROPE_THETA ** (jnp.arange(0, AXIS_DIM, 2, dtype=jnp.float32) / AXIS_DIM))
    ang = jnp.stack([row.astype(jnp.float32)[:, None] * inv,
                     col.astype(jnp.float32)[:, None] * inv], axis=1)
    s = ang.shape[0]
    ang = jnp.broadcast_to(ang[:, :, None, :], (s, 2, 2, AXIS_DIM // 2)).reshape(s, HEAD_DIM)
    return jnp.cos(ang).astype(dtype), jnp.sin(ang).astype(dtype)


def apply_rope(x, cos, sin):
    xr = x.reshape(x.shape[:-1] + (2, 2, AXIS_DIM // 2))
    rot = jnp.stack([-xr[..., 1, :], xr[..., 0, :]], axis=-2).reshape(x.shape)
    return x * cos[None, :, None, :] + rot * sin[None, :, None, :]


def split_heads(t, n):
    return t.reshape(t.shape[:-1] + (n, HEAD_DIM))


def conv_mixer(u, bg, cg, w):
    z = cg * u
    zp = jnp.pad(z, ((0, 0), (1, 1), (0, 0)))
    z = zp[:, :-2] * w[0] + zp[:, 1:-1] * w[1] + zp[:, 2:] * w[2]
    return bg * z


def fourier_mixer(f):
    b, L, _ = f.shape
    fh = f.reshape(b, L, FOUR_GROUPS, FOUR_GROUP_DIM).astype(jnp.float32)
    y = jnp.fft.fft2(fh, axes=(1, 3), norm="ortho").real
    return y.reshape(b, L, FOUR_W).astype(f.dtype)


def attend(qg, k, v):
    s = jnp.einsum('bqkgd,blkd->bkgql', qg, k).astype(jnp.float32) * (HEAD_DIM ** -0.5)
    p = jax.nn.softmax(s, axis=-1).astype(v.dtype)
    return jnp.einsum('bkgql,blkd->bqkgd', p, v)


def latent_attention(q, k_all, v_all):
    b, s = q.shape[:2]
    nb = s // Q_BLOCK
    qb = q.reshape(b, nb, Q_BLOCK, N_KV, GROUP, HEAD_DIM).transpose(1, 0, 2, 3, 4, 5)
    ob = lax.map(lambda blk: attend(blk, k_all, v_all), qb)
    return ob.transpose(1, 0, 2, 3, 4, 5).reshape(b, s, ATT_W)


def context_kv(pkv, kg):
    kc = rms_norm(split_heads(pkv[..., :KV_W], N_KV), kg)
    vc = split_heads(pkv[..., KV_W:], N_KV)
    return kc, vc


def squared_relu_mlp(h, w_up, w_down):
    a = jax.nn.relu(h @ w_up)
    return (a * a) @ w_down


def setup_inputs(seed: int = 0) -> dict:
    key = jax.random.key(seed)
    ks = jax.random.split(key, 20)
    f32 = jnp.float32
    D = D_MODEL
    nrm = lambda k, shp, s: jax.random.normal(k, shp, f32) * s
    return {
        "x": nrm(ks[0], (BATCH, SEQ, D), 1.0),
        "c": nrm(ks[1], (BATCH, D), 1.0),
        "ctx": nrm(ks[2], (BATCH, CTX_LEN, D), 1.0),
        "c_ctx": nrm(ks[3], (D,), 1.0),
        "w_ada": nrm(ks[4], (DEPTH, D, N_MOD * D), D ** -0.5),
        "b_ada": nrm(ks[5], (DEPTH, N_MOD * D), 0.01),
        "w_in": nrm(ks[6], (DEPTH, D, PROJ_W), D ** -0.5),
        "conv_w": nrm(ks[7], (DEPTH, 3, CONV_W), 3 ** -0.5),
        "q_gain": 1.0 + nrm(ks[8], (DEPTH, HEAD_DIM), 0.02),
        "k_gain": 1.0 + nrm(ks[9], (DEPTH, HEAD_DIM), 0.02),
        "w_out": nrm(ks[10], (DEPTH, MIX_W, D), MIX_W ** -0.5),
        "g_pre_mix": 1.0 + nrm(ks[11], (DEPTH, D), 0.02),
        "g_post_mix": 1.0 + nrm(ks[12], (DEPTH, D), 0.02),
        "g_pre_ffn": 1.0 + nrm(ks[13], (DEPTH, D), 0.02),
        "g_post_ffn": 1.0 + nrm(ks[14], (DEPTH, D), 0.02),
        "w_ffn_up": nrm(ks[15], (DEPTH, D, D_FF), D ** -0.5),
        "w_ffn_down": nrm(ks[16], (DEPTH, D_FF, D), D_FF ** -0.5),
    }


def reference(x, c, ctx, c_ctx, w_ada, b_ada, w_in, conv_w, q_gain, k_gain, w_out,
              g_pre_mix, g_post_mix, g_pre_ffn, g_post_ffn, w_ffn_up, w_ffn_down):
    b, s, d = x.shape
    rows = s // GRID_W
    row = jnp.broadcast_to(jnp.arange(rows)[:, None], (rows, GRID_W)).reshape(-1)
    col = jnp.broadcast_to(jnp.arange(GRID_W)[None, :], (rows, GRID_W)).reshape(-1)
    cos, sin = rope_tables(row, col, x.dtype)
    sc_lat = jax.nn.silu(c)[:, None, :]
    sc_ctx = jax.nn.silu(c_ctx)
    xc = ctx
    for l in range(DEPTH):
        last = l == DEPTH - 1
        sh1, sc1, gt1, sh2, sc2, gt2 = jnp.split(sc_lat @ w_ada[l] + b_ada[l], N_MOD, axis=-1)
        csh1, csc1, cgt1, csh2, csc2, cgt2 = jnp.split(sc_ctx @ w_ada[l] + b_ada[l], N_MOD, axis=-1)

        hc = modulate(rms_norm(xc, g_pre_mix[l]), csh1, csc1)
        if last:
            kc, vc = context_kv(hc @ w_in[l][:, K_START:], k_gain[l])
        else:
            pc = hc @ w_in[l]
            kc, vc = context_kv(pc[..., K_START:], k_gain[l])
            qc = rms_norm(split_heads(pc[..., Q_START:K_START], N_HEADS), q_gain[l])
            qc = qc.reshape(qc.shape[:2] + (N_KV, GROUP, HEAD_DIM))
            att_c = attend(qc, kc, vc).reshape(b, -1, ATT_W)
            yc = jnp.concatenate([
                conv_mixer(pc[..., :CONV_W], pc[..., CONV_W:2 * CONV_W],
                           pc[..., 2 * CONV_W:3 * CONV_W], conv_w[l]),
                fourier_mixer(pc[..., 3 * CONV_W:Q_START]),
                att_c], axis=-1) @ w_out[l]
            xc_mid = xc + cgt1 * rms_norm(yc, g_post_mix[l])
            hc2 = modulate(rms_norm(xc_mid, g_pre_ffn[l]), csh2, csc2)
            xc_next = xc_mid + cgt2 * rms_norm(squared_relu_mlp(hc2, w_ffn_up[l], w_ffn_down[l]), g_post_ffn[l])

        h = modulate(rms_norm(x, g_pre_mix[l]), sh1, sc1)
        p = h @ w_in[l]
        q = apply_rope(rms_norm(split_heads(p[..., Q_START:K_START], N_HEADS), q_gain[l]), cos, sin)
        k = apply_rope(rms_norm(split_heads(p[..., K_START:V_START], N_KV), k_gain[l]), cos, sin)
        v = split_heads(p[..., V_START:], N_KV)
        k_all = jnp.concatenate([kc, k], axis=1)
        v_all = jnp.concatenate([vc, v], axis=1)
        y = jnp.concatenate([
            conv_mixer(p[..., :CONV_W], p[..., CONV_W:2 * CONV_W],
                       p[..., 2 * CONV_W:3 * CONV_W], conv_w[l]),
            fourier_mixer(p[..., 3 * CONV_W:Q_START]),
            latent_attention(q, k_all, v_all)], axis=-1) @ w_out[l]
        x = x + gt1 * rms_norm(y, g_post_mix[l])
        h2 = modulate(rms_norm(x, g_pre_ffn[l]), sh2, sc2)
        x = x + gt2 * rms_norm(squared_relu_mlp(h2, w_ffn_up[l], w_ffn_down[l]), g_post_ffn[l])

        if not last:
            xc = xc_next
    return x
```

```python
import functools
import math

import jax
import jax.numpy as jnp
import numpy as np
from jax import lax
from jax.experimental import pallas as pl
from jax.experimental.pallas import tpu as pltpu

F32 = jnp.float32
BF16 = jnp.bfloat16

D_MODEL = 1024
BATCH = 4
SEQ = 4096
DEPTH = 4
GRID_W = 64
CTX_LEN = 256
CONV_W = 256
FOUR_GROUPS = 4
FOUR_GROUP_DIM = 64
FOUR_W = FOUR_GROUPS * FOUR_GROUP_DIM
N_HEADS = 8
N_KV = 2
GROUP = N_HEADS // N_KV
HEAD_DIM = 64
ATT_W = N_HEADS * HEAD_DIM
KV_W = N_KV * HEAD_DIM
Q_START = 3 * CONV_W + FOUR_W
K_START = Q_START + ATT_W
V_START = K_START + KV_W
PROJ_W = V_START + KV_W
AXIS_DIM = HEAD_DIM // 2
ROPE_THETA = 10000.0
D_FF = 4 * D_MODEL
N_MOD = 6
EPS = 1e-6

N_LAT = BATCH * SEQ
N_CTX = BATCH * CTX_LEN
N_TOK = N_LAT + N_CTX
MOD_ROWS = 8
CTX_MOD_ROW = BATCH

LANES = 128
BF16_SUBLANES = 16
VMEM_LIMIT_BYTES = 56 * 1024 * 1024

TM = 512
TQ = 128
F1_COLS = 4096
DFT_R = 64
F2_GROUP = 8
HALO = BF16_SUBLANES

assert SEQ == DFT_R * DFT_R and SEQ % TM == 0 and N_CTX % TM == 0 and CTX_LEN % TQ == 0


def _params(n_axes):
    return pltpu.CompilerParams(dimension_semantics=("arbitrary",) * n_axes,
                                vmem_limit_bytes=VMEM_LIMIT_BYTES)


def _full(shape):
    return pl.BlockSpec(shape, lambda *_: (0,) * len(shape))


def _mod_row(i):
    return jnp.where(i < N_LAT // TM, i // (SEQ // TM), CTX_MOD_ROW)


def _ada_kernel(c_ref, w_ref, b_ref, o_ref):
    c = c_ref[...]
    s = c * jax.nn.sigmoid(c)
    o_ref[...] = jnp.dot(s, w_ref[...], preferred_element_type=F32,
                         precision=lax.Precision.HIGHEST) + b_ref[...]


def _ada(c8, w_ada, b_ada):
    tn = 1024
    n = N_MOD * D_MODEL
    return pl.pallas_call(
        _ada_kernel,
        grid=(DEPTH, n // tn),
        in_specs=[pl.BlockSpec((MOD_ROWS, D_MODEL), lambda l, j: (0, 0)),
                  pl.BlockSpec((None, D_MODEL, tn), lambda l, j: (l, 0, j)),
                  pl.BlockSpec((None, 1, tn), lambda l, j: (l, 0, j))],
        out_specs=pl.BlockSpec((None, MOD_ROWS, tn), lambda l, j: (l, 0, j)),
        out_shape=jax.ShapeDtypeStruct((DEPTH, MOD_ROWS, n), F32),
        compiler_params=_params(2),
        name="ada",
    )(c8, w_ada, b_ada.reshape(DEPTH, 1, n))


def _rms(x, g):
    return x * lax.rsqrt(jnp.mean(x * x, axis=-1, keepdims=True) + EPS) * g


def _head_norm_rope(t, ones, gain, cos, sin_lo, sin_hi):
    w = t.shape[-1]
    ms = jnp.dot((t * t).astype(BF16), ones, preferred_element_type=F32)
    tn = t * lax.rsqrt(ms + EPS) * gain
    up = pltpu.roll(tn, w - AXIS_DIM // 2, 1)
    dn = pltpu.roll(tn, AXIS_DIM // 2, 1)
    return tn * cos + up * sin_lo + dn * sin_hi


def _inproj_kernel(x_ref, mod_ref, g_ref, w_ref, mch_ref, ones_ref, qg_ref, kg_ref,
                   cos_ref, slo_ref, shi_ref,
                   ubc_ref, zre_ref, zim_ref, q_ref, kt_ref, v_ref):
    h = _rms(x_ref[...], g_ref[...]) * (1.0 + mod_ref[1:2, :]) + mod_ref[0:1, :]
    p = jnp.dot(h.astype(BF16), w_ref[...], preferred_element_type=F32)
    ubc_ref[...] = p[:, :3 * CONV_W].astype(BF16)
    z = jnp.dot(p[:, 3 * CONV_W:Q_START].astype(BF16), mch_ref[...], preferred_element_type=F32)
    zre_ref[...] = z[:, :FOUR_W].astype(BF16)
    zim_ref[...] = z[:, FOUR_W:].astype(BF16)
    cos, slo, shi = cos_ref[...], slo_ref[...], shi_ref[...]
    ones = ones_ref[...]
    rep = ATT_W // LANES
    q = _head_norm_rope(p[:, Q_START:K_START], ones, qg_ref[...],
                        jnp.tile(cos, (1, rep)), jnp.tile(slo, (1, rep)), jnp.tile(shi, (1, rep)))
    q_ref[...] = (q * HEAD_DIM ** -0.5).astype(BF16)
    k = _head_norm_rope(p[:, K_START:V_START], ones_ref[:KV_W, :KV_W], kg_ref[...], cos, slo, shi)
    kt_ref[...] = k.T.astype(BF16)
    v_ref[...] = p[:, V_START:].astype(BF16)


def _inproj(x, mod_l, g, w, mch, ones, qg, kg, cos, slo, shi):
    n_tiles = N_TOK // TM
    n_lat_tiles = N_LAT // TM
    pos_tiles = SEQ // TM

    def pos_map(i):
        return (jnp.where(i < n_lat_tiles, i % pos_tiles, pos_tiles), 0)

    row = lambda i: (i, 0)
    pos_spec = pl.BlockSpec((TM, LANES), pos_map)
    return pl.pallas_call(
        _inproj_kernel,
        grid=(n_tiles,),
        in_specs=[pl.BlockSpec((TM, D_MODEL), row),
                  pl.BlockSpec((None, N_MOD, D_MODEL), lambda i: (_mod_row(i), 0, 0)),
                  _full((1, D_MODEL)),
                  _full((D_MODEL, PROJ_W)),
                  _full((FOUR_W, 2 * FOUR_W)),
                  _full((ATT_W, ATT_W)),
                  _full((1, ATT_W)),
                  _full((1, KV_W)),
                  pos_spec, pos_spec, pos_spec],
        out_specs=[pl.BlockSpec((TM, 3 * CONV_W), row),
                   pl.BlockSpec((TM, FOUR_W), row),
                   pl.BlockSpec((TM, FOUR_W), row),
                   pl.BlockSpec((TM, ATT_W), row),
                   pl.BlockSpec((KV_W, TM), lambda i: (0, i)),
                   pl.BlockSpec((TM, KV_W), row)],
        out_shape=[jax.ShapeDtypeStruct((N_TOK, 3 * CONV_W), BF16),
                   jax.ShapeDtypeStruct((N_TOK, FOUR_W), BF16),
                   jax.ShapeDtypeStruct((N_TOK, FOUR_W), BF16),
                   jax.ShapeDtypeStruct((N_TOK, ATT_W), BF16),
                   jax.ShapeDtypeStruct((KV_W, N_TOK), BF16),
                   jax.ShapeDtypeStruct((N_TOK, KV_W), BF16)],
        compiler_params=_params(1),
        name="in_proj",
    )(x, mod_l, g, w, mch, ones, qg, kg, cos, slo, shi)


def _attn_heads(q_ref, segments, o_ref):
    tq = q_ref.shape[0]
    for h in range(N_KV):
        hs = slice(h * HEAD_DIM, (h + 1) * HEAD_DIM)
        qh = jnp.concatenate(
            [q_ref[:, (h * GROUP + g) * HEAD_DIM:(h * GROUP + g + 1) * HEAD_DIM] for g in range(GROUP)],
            axis=0)
        scores = [jnp.dot(qh, kt_ref[hs, :], preferred_element_type=F32) for kt_ref, _ in segments]
        m = functools.reduce(jnp.maximum, [jnp.max(s, axis=-1, keepdims=True) for s in scores])
        probs = [jnp.exp(s - m) for s in scores]
        denom = functools.reduce(jnp.add, [jnp.sum(pr, axis=-1, keepdims=True) for pr in probs])
        o = functools.reduce(jnp.add, [
            jnp.dot(pr.astype(BF16), v_ref[:, hs], preferred_element_type=F32)
            for pr, (_, v_ref) in zip(probs, segments)])
        o = o / denom
        for g in range(GROUP):
            c0 = (h * GROUP + g) * HEAD_DIM
            o_ref[:, c0:c0 + HEAD_DIM] = o[g * tq:(g + 1) * tq].astype(o_ref.dtype)


def _attn_lat_kernel(q_ref, ktl_ref, ktc_ref, vl_ref, vc_ref, o_ref):
    _attn_heads(q_ref, [(ktc_ref, vc_ref), (ktl_ref, vl_ref)], o_ref)


def _attn_ctx_kernel(q_ref, ktc_ref, vc_ref, prev_ref, o_ref):
    del prev_ref
    _attn_heads(q_ref, [(ktc_ref, vc_ref)], o_ref)


def _attn_lat(q, kt, v):
    nq = SEQ // TQ
    ctx0 = N_LAT // CTX_LEN
    return pl.pallas_call(
        _attn_lat_kernel,
        grid=(BATCH, nq),
        in_specs=[pl.BlockSpec((TQ, ATT_W), lambda b, j: (b * nq + j, 0)),
                  pl.BlockSpec((KV_W, SEQ), lambda b, j: (0, b)),
                  pl.BlockSpec((KV_W, CTX_LEN), lambda b, j: (0, ctx0 + b)),
                  pl.BlockSpec((SEQ, KV_W), lambda b, j: (b, 0)),
                  pl.BlockSpec((CTX_LEN, KV_W), lambda b, j: (ctx0 + b, 0))],
        out_specs=pl.BlockSpec((TQ, ATT_W), lambda b, j: (b * nq + j, 0)),
        out_shape=jax.ShapeDtypeStruct((N_TOK, ATT_W), BF16),
        compiler_params=_params(2),
        name="attn_latent",
    )(q, kt, kt, v, v)


def _attn_ctx(q, kt, v, att):
    nq = CTX_LEN // TQ
    q0 = N_LAT // TQ
    ctx0 = N_LAT // CTX_LEN
    return pl.pallas_call(
        _attn_ctx_kernel,
        grid=(BATCH, nq),
        in_specs=[pl.BlockSpec((TQ, ATT_W), lambda b, j: (q0 + b * nq + j, 0)),
                  pl.BlockSpec((KV_W, CTX_LEN), lambda b, j: (0, ctx0 + b)),
                  pl.BlockSpec((CTX_LEN, KV_W), lambda b, j: (ctx0 + b, 0)),
                  pl.BlockSpec(memory_space=pl.ANY)],
        out_specs=pl.BlockSpec((TQ, ATT_W), lambda b, j: (q0 + b * nq + j, 0)),
        out_shape=jax.ShapeDtypeStruct((N_TOK, ATT_W), BF16),
        input_output_aliases={3: 0},
        compiler_params=_params(2),
        name="attn_context",
    )(q, kt, v, att)


def _four1_kernel(zre_ref, zim_ref, c_ref, s_ref, are_ref, aim_ref):
    zre, zim, c, s = zre_ref[...], zim_ref[...], c_ref[...], s_ref[...]
    dot = functools.partial(jnp.dot, preferred_element_type=F32)
    are_ref[...] = (dot(c, zre) + dot(s, zim)).astype(BF16)
    aim_ref[...] = (dot(c, zim) - dot(s, zre)).astype(BF16)


def _four1(zre, zim, c64, s64):
    cols = DFT_R * FOUR_W
    zre_v = zre.reshape(N_TOK // DFT_R, cols)
    zim_v = zim.reshape(N_TOK // DFT_R, cols)
    blk = pl.BlockSpec((DFT_R, F1_COLS), lambda b, j: (b, j))
    out = jax.ShapeDtypeStruct((BATCH * DFT_R, cols), BF16)
    return pl.pallas_call(
        _four1_kernel,
        grid=(BATCH, cols // F1_COLS),
        in_specs=[blk, blk, _full((DFT_R, DFT_R)), _full((DFT_R, DFT_R))],
        out_specs=[blk, blk],
        out_shape=[out, out],
        compiler_params=_params(2),
        name="fourier_stage1",
    )(zre_v, zim_v, c64, s64)


def _four2_kernel(are_ref, aim_ref, tc_ref, ts_ref, o_ref):
    y = (jnp.dot(tc_ref[...], are_ref[...], preferred_element_type=F32)
         + jnp.dot(ts_ref[...], aim_ref[...], preferred_element_type=F32))
    o_ref[...] = y.reshape(DFT_R, F2_GROUP, FOUR_W)


def _four2(are, aim, tc, ts):
    n_groups = DFT_R // F2_GROUP
    rows = F2_GROUP * DFT_R
    are_v = are.reshape(N_LAT, FOUR_W)
    aim_v = aim.reshape(N_LAT, FOUR_W)
    a_spec = pl.BlockSpec((rows, FOUR_W), lambda m, b: (b * n_groups + m, 0))
    t_spec = pl.BlockSpec((None, rows, rows), lambda m, b: (m, 0, 0))
    yf = pl.pallas_call(
        _four2_kernel,
        grid=(n_groups, BATCH),
        in_specs=[a_spec, a_spec, t_spec, t_spec],
        out_specs=pl.BlockSpec((DFT_R, None, F2_GROUP, FOUR_W), lambda m, b: (b, m, 0, 0)),
        out_shape=jax.ShapeDtypeStruct((N_TOK // DFT_R, n_groups, F2_GROUP, FOUR_W), F32),
        compiler_params=_params(2),
        name="fourier_stage2",
    )(are_v, aim_v, tc, ts)
    return yf.reshape(N_TOK, FOUR_W)


def _four_ctx_kernel(zre_ref, zim_ref, c_ref, s_ref, prev_ref, o_ref):
    del prev_ref
    o_ref[...] = (jnp.dot(c_ref[...], zre_ref[...], preferred_element_type=F32)
                  + jnp.dot(s_ref[...], zim_ref[...], preferred_element_type=F32))


def _four_ctx(zre, zim, c256, s256, yf):
    ctx0 = N_LAT // CTX_LEN
    blk = pl.BlockSpec((CTX_LEN, FOUR_W), lambda b: (ctx0 + b, 0))
    return pl.pallas_call(
        _four_ctx_kernel,
        grid=(BATCH,),
        in_specs=[blk, blk, _full((CTX_LEN, CTX_LEN)), _full((CTX_LEN, CTX_LEN)),
                  pl.BlockSpec(memory_space=pl.ANY)],
        out_specs=blk,
        out_shape=jax.ShapeDtypeStruct((N_TOK, FOUR_W), F32),
        input_output_aliases={4: 0},
        compiler_params=_params(1),
        name="fourier_context",
    )(zre, zim, c256, s256, yf)


def _outproj_kernel(ubc_ref, hp_ref, hn_ref, yf_ref, att_ref, x_ref, mod_ref, g_ref, cw_ref, wo_ref, o_ref):
    i = pl.program_id(0)
    ubc = ubc_ref[...].astype(F32)
    u, bg, cg = ubc[:, :CONV_W], ubc[:, CONV_W:2 * CONV_W], ubc[:, 2 * CONV_W:]
    z = cg * u
    hp = hp_ref[HALO - 1:HALO, :].astype(F32)
    hn = hn_ref[0:1, :].astype(F32)
    z_before = hp[:, 2 * CONV_W:] * hp[:, :CONV_W]
    z_after = hn[:, 2 * CONV_W:] * hn[:, :CONV_W]
    row = lax.broadcasted_iota(jnp.int32, (TM, 1), 0)
    z_prev = jnp.where(row == 0, z_before, pltpu.roll(z, 1, 0))
    z_next = jnp.where(row == TM - 1, z_after, pltpu.roll(z, TM - 1, 0))
    tok = i * TM + row
    seg = jnp.where(tok < N_LAT, SEQ, CTX_LEN)
    pos = jnp.bitwise_and(tok, seg - 1)
    z_prev = jnp.where(pos == 0, 0.0, z_prev)
    z_next = jnp.where(pos == seg - 1, 0.0, z_next)
    conv = bg * (z_prev * cw_ref[0:1, :] + z * cw_ref[1:2, :] + z_next * cw_ref[2:3, :])
    dot = functools.partial(jnp.dot, preferred_element_type=F32)
    y = (dot(conv.astype(BF16), wo_ref[:CONV_W, :])
         + dot(yf_ref[...].astype(BF16), wo_ref[CONV_W:CONV_W + FOUR_W, :])
         + dot(att_ref[...], wo_ref[CONV_W + FOUR_W:, :]))
    o_ref[...] = x_ref[...] + mod_ref[2:3, :] * _rms(y, g_ref[...])


def _outproj(n_rows, ubc, yf, att, x, mod_l, g, cw, wo):
    n_tiles = n_rows // TM
    per = TM // HALO
    last_halo = n_rows // HALO - 1
    row = lambda i: (i, 0)
    return pl.pallas_call(
        _outproj_kernel,
        grid=(n_tiles,),
        in_specs=[pl.BlockSpec((TM, 3 * CONV_W), row),
                  pl.BlockSpec((HALO, 3 * CONV_W), lambda i: (jnp.maximum(i * per - 1, 0), 0)),
                  pl.BlockSpec((HALO, 3 * CONV_W), lambda i: (jnp.minimum((i + 1) * per, last_halo), 0)),
                  pl.BlockSpec((TM, FOUR_W), row),
                  pl.BlockSpec((TM, ATT_W), row),
                  pl.BlockSpec((TM, D_MODEL), row),
                  pl.BlockSpec((None, N_MOD, D_MODEL), lambda i: (_mod_row(i), 0, 0)),
                  _full((1, D_MODEL)),
                  _full((3, CONV_W)),
                  _full((MIX_W_ROWS, D_MODEL))],
        out_specs=pl.BlockSpec((TM, D_MODEL), row),
        out_shape=jax.ShapeDtypeStruct((n_rows, D_MODEL), F32),
        compiler_params=_params(1),
        name="out_proj",
    )(ubc, ubc, ubc, yf, att, x, mod_l, g, cw, wo)


MIX_W_ROWS = CONV_W + FOUR_W + ATT_W


FF_CHUNK = 1024


def _ffn_kernel(x_ref, mod_ref, gpre_ref, gpost_ref, wu_ref, wd_ref, o_ref):
    x = x_ref[...]
    h = (_rms(x, gpre_ref[...]) * (1.0 + mod_ref[4:5, :]) + mod_ref[3:4, :]).astype(BF16)
    acc = jnp.zeros((TM, D_MODEL), F32)
    for c in range(D_FF // FF_CHUNK):
        cs = slice(c * FF_CHUNK, (c + 1) * FF_CHUNK)
        a = jnp.maximum(jnp.dot(h, wu_ref[:, cs], preferred_element_type=F32), 0.0)
        acc = acc + jnp.dot((a * a).astype(BF16), wd_ref[cs, :], preferred_element_type=F32)
    o_ref[...] = x + mod_ref[5:6, :] * _rms(acc, gpost_ref[...])


def _ffn(n_rows, x, mod_l, gpre, gpost, wu, wd):
    row = lambda i: (i, 0)
    return pl.pallas_call(
        _ffn_kernel,
        grid=(n_rows // TM,),
        in_specs=[pl.BlockSpec((TM, D_MODEL), row),
                  pl.BlockSpec((None, N_MOD, D_MODEL), lambda i: (_mod_row(i), 0, 0)),
                  _full((1, D_MODEL)),
                  _full((1, D_MODEL)),
                  _full((D_MODEL, D_FF)),
                  _full((D_FF, D_MODEL))],
        out_specs=pl.BlockSpec((TM, D_MODEL), row),
        out_shape=jax.ShapeDtypeStruct((n_rows, D_MODEL), F32),
        compiler_params=_params(1),
        name="ffn",
    )(x, mod_l, gpre, gpost, wu, wd)


def _rope_tables():
    t = jnp.arange(SEQ)
    rowp = (t // GRID_W).astype(F32)
    colp = (t % GRID_W).astype(F32)
    inv = 1.0 / (ROPE_THETA ** (jnp.arange(0, AXIS_DIM, 2, dtype=F32) / AXIS_DIM))
    ang = jnp.stack([rowp[:, None] * inv, colp[:, None] * inv], axis=1)
    ang = jnp.broadcast_to(ang[:, :, None, :], (SEQ, 2, 2, AXIS_DIM // 2)).reshape(SEQ, HEAD_DIM)
    cos, sin = jnp.cos(ang), jnp.sin(ang)
    first_half = (jnp.arange(HEAD_DIM) % AXIS_DIM) < AXIS_DIM // 2
    sin_lo = jnp.where(first_half, -sin, 0.0)
    sin_hi = jnp.where(first_half, 0.0, sin)
    ident = lambda v: jnp.full((TM, HEAD_DIM), v, F32)
    wide = lambda a, v: jnp.tile(jnp.concatenate([a, ident(v)], axis=0), (1, LANES // HEAD_DIM))
    return wide(cos, 1.0), wide(sin_lo, 0.0), wide(sin_hi, 0.0)


def _dft_tables():
    two_pi = 2.0 * np.pi
    n = np.arange(FOUR_GROUP_DIM)
    ang = two_pi * np.outer(n, n) / FOUR_GROUP_DIM
    eye = np.eye(FOUR_GROUPS)
    scale = FOUR_GROUP_DIM ** -0.5
    mch = np.concatenate([np.kron(eye, np.cos(ang)), -np.kron(eye, np.sin(ang))], axis=1) * scale
    r = np.arange(DFT_R)
    ang_r = two_pi * np.outer(r, r) / DFT_R
    c64, s64 = np.cos(ang_r), np.sin(ang_r)
    n_groups = DFT_R // F2_GROUP
    m = np.arange(n_groups)[:, None, None, None]
    k2 = np.arange(DFT_R)[None, :, None, None]
    g = np.arange(F2_GROUP)[None, None, :, None]
    c = np.arange(DFT_R)[None, None, None, :]
    k = F2_GROUP * m + g + DFT_R * k2
    ang2 = two_pi * ((k * c) % SEQ) / SEQ
    sel = np.eye(F2_GROUP)[None, None, :, :, None]
    norm2 = SEQ ** -0.5
    tc = (np.cos(ang2)[:, :, :, None, :] * sel * norm2).reshape(n_groups, DFT_R * F2_GROUP, F2_GROUP * DFT_R)
    ts = (np.sin(ang2)[:, :, :, None, :] * sel * norm2).reshape(n_groups, DFT_R * F2_GROUP, F2_GROUP * DFT_R)
    t = np.arange(CTX_LEN)
    ang_c = two_pi * (np.outer(t, t) % CTX_LEN) / CTX_LEN
    normc = CTX_LEN ** -0.5
    c256, s256 = np.cos(ang_c) * normc, np.sin(ang_c) * normc
    as_bf16 = lambda a: jnp.asarray(a, dtype=F32).astype(BF16)
    return tuple(as_bf16(a) for a in (mch, c64, s64, tc, ts, c256, s256))


def _head_mean_matrix():
    blocks = np.kron(np.eye(N_HEADS), np.full((HEAD_DIM, HEAD_DIM), 1.0 / HEAD_DIM))
    return jnp.asarray(blocks, dtype=F32).astype(BF16)


def kernel(x, c, ctx, c_ctx, w_ada, b_ada, w_in, conv_w, q_gain, k_gain, w_out,
           g_pre_mix, g_post_mix, g_pre_ffn, g_post_ffn, w_ffn_up, w_ffn_down):
    cos, sin_lo, sin_hi = _rope_tables()
    mch, c64, s64, tc, ts, c256, s256 = _dft_tables()
    ones = _head_mean_matrix()

    c8 = jnp.concatenate([c, c_ctx[None, :], jnp.zeros((MOD_ROWS - BATCH - 1, D_MODEL), F32)], axis=0)
    mod = _ada(c8, w_ada, b_ada).reshape(DEPTH, MOD_ROWS, N_MOD, D_MODEL)

    xs = jnp.concatenate([x.reshape(N_LAT, D_MODEL), ctx.reshape(N_CTX, D_MODEL)], axis=0)
    w_in_b, w_out_b = w_in.astype(BF16), w_out.astype(BF16)
    w_up_b, w_down_b = w_ffn_up.astype(BF16), w_ffn_down.astype(BF16)
    vec = lambda a: a.reshape(1, -1)

    for l in range(DEPTH):
        last = l == DEPTH - 1
        qg = vec(jnp.tile(q_gain[l], N_HEADS))
        kg = vec(jnp.tile(k_gain[l], N_KV))
        ubc, zre, zim, q, kt, v = _inproj(xs, mod[l], vec(g_pre_mix[l]), w_in_b[l], mch, ones,
                                          qg, kg, cos, sin_lo, sin_hi)
        att = _attn_lat(q, kt, v)
        are, aim = _four1(zre, zim, c64, s64)
        yf = _four2(are, aim, tc, ts)
        if last:
            n_rows = N_LAT
        else:
            n_rows = N_TOK
            att = _attn_ctx(q, kt, v, att)
            yf = _four_ctx(zre, zim, c256, s256, yf)
        xs = _outproj(n_rows, ubc, yf, att, xs, mod[l], vec(g_post_mix[l]), conv_w[l], w_out_b[l])
        xs = _ffn(n_rows, xs, mod[l], vec(g_pre_ffn[l]), vec(g_post_ffn[l]), w_up_b[l], w_down_b[l])
    return xs.reshape(BATCH, SEQ, D_MODEL)
```

```python
import functools
import math

import jax
import jax.numpy as jnp
import numpy as np
from jax import lax
from jax.experimental import pallas as pl
from jax.experimental.pallas import tpu as pltpu

F32 = jnp.float32
BF16 = jnp.bfloat16

D_MODEL = 1024
BATCH = 4
SEQ = 4096
DEPTH = 4
GRID_W = 64
CTX_LEN = 256
CONV_W = 256
FOUR_GROUPS = 4
FOUR_GROUP_DIM = 64
FOUR_W = FOUR_GROUPS * FOUR_GROUP_DIM
N_HEADS = 8
N_KV = 2
GROUP = N_HEADS // N_KV
HEAD_DIM = 64
ATT_W = N_HEADS * HEAD_DIM
KV_W = N_KV * HEAD_DIM
Q_START = 3 * CONV_W + FOUR_W
K_START = Q_START + ATT_W
V_START = K_START + KV_W
PROJ_W = V_START + KV_W
AXIS_DIM = HEAD_DIM // 2
ROPE_THETA = 10000.0
D_FF = 4 * D_MODEL
N_MOD = 6
EPS = 1e-6

N_LAT = BATCH * SEQ
N_CTX = BATCH * CTX_LEN
N_TOK = N_LAT + N_CTX
MOD_ROWS = 8
CTX_MOD_ROW = BATCH

LANES = 128
BF16_SUBLANES = 16
VMEM_LIMIT_BYTES = 56 * 1024 * 1024

TM = 512
TQ = 256
KT = 256
LOG2_E = math.log2(math.e)
MIX_W_ROWS = CONV_W + FOUR_W + ATT_W
F1_COLS = 4096
DFT_R = 64
F2_GROUP = 8
HALO = BF16_SUBLANES

assert SEQ == DFT_R * DFT_R and SEQ % TM == 0 and N_CTX % TM == 0
assert CTX_LEN == TQ == KT and SEQ % TQ == 0 and TM % KT == 0


def _params(n_axes):
    return pltpu.CompilerParams(dimension_semantics=("arbitrary",) * n_axes,
                                vmem_limit_bytes=VMEM_LIMIT_BYTES)


def _full(shape):
    return pl.BlockSpec(shape, lambda *_: (0,) * len(shape))


def _mod_row(i):
    return jnp.where(i < N_LAT // TM, i // (SEQ // TM), CTX_MOD_ROW)


def _ada_kernel(c_ref, w_ref, b_ref, o_ref):
    c = c_ref[...]
    s = c * jax.nn.sigmoid(c)
    o_ref[...] = jnp.dot(s, w_ref[...], preferred_element_type=F32,
                         precision=lax.Precision.HIGHEST) + b_ref[...]


def _ada(c8, w_ada, b_ada):
    tn = 1024
    n = N_MOD * D_MODEL
    return pl.pallas_call(
        _ada_kernel,
        grid=(DEPTH, n // tn),
        in_specs=[pl.BlockSpec((MOD_ROWS, D_MODEL), lambda l, j: (0, 0)),
                  pl.BlockSpec((None, D_MODEL, tn), lambda l, j: (l, 0, j)),
                  pl.BlockSpec((None, 1, tn), lambda l, j: (l, 0, j))],
        out_specs=pl.BlockSpec((None, MOD_ROWS, tn), lambda l, j: (l, 0, j)),
        out_shape=jax.ShapeDtypeStruct((DEPTH, MOD_ROWS, n), F32),
        compiler_params=_params(2),
        name="ada",
    )(c8, w_ada, b_ada.reshape(DEPTH, 1, n))


def _rms(x, g):
    return x * lax.rsqrt(jnp.mean(x * x, axis=-1, keepdims=True) + EPS) * g


def _head_norm_rope(t, ones, gain, cos, sin_lo, sin_hi):
    w = t.shape[-1]
    ms = jnp.dot((t * t).astype(BF16), ones, preferred_element_type=F32)
    tn = t * lax.rsqrt(ms + EPS) * gain
    up = pltpu.roll(tn, w - AXIS_DIM // 2, 1)
    dn = pltpu.roll(tn, AXIS_DIM // 2, 1)
    return tn * cos + up * sin_lo + dn * sin_hi


def _inproj_kernel(x_ref, mod_ref, g_ref, w_ref, mch_ref, ones_ref, qg_ref, kg_ref,
                   cos_ref, slo_ref, shi_ref,
                   ubc_ref, zre_ref, zim_ref, qt_ref, k_ref, vt_ref):
    h = _rms(x_ref[...], g_ref[...]) * (1.0 + mod_ref[1:2, :]) + mod_ref[0:1, :]
    p = jnp.dot(h.astype(BF16), w_ref[...], preferred_element_type=F32)
    ubc_ref[...] = p[:, :3 * CONV_W].astype(BF16)
    z = jnp.dot(p[:, 3 * CONV_W:Q_START].astype(BF16), mch_ref[...], preferred_element_type=F32)
    zre_ref[...] = z[:, :FOUR_W].astype(BF16)
    zim_ref[...] = z[:, FOUR_W:].astype(BF16)
    cos, slo, shi = cos_ref[...], slo_ref[...], shi_ref[...]
    ones = ones_ref[...]
    rep = ATT_W // LANES
    q = _head_norm_rope(p[:, Q_START:K_START], ones, qg_ref[...],
                        jnp.tile(cos, (1, rep)), jnp.tile(slo, (1, rep)), jnp.tile(shi, (1, rep)))
    qt_ref[...] = (q * (HEAD_DIM ** -0.5 * LOG2_E)).T.astype(BF16)
    k = _head_norm_rope(p[:, K_START:V_START], ones_ref[:KV_W, :KV_W], kg_ref[...], cos, slo, shi)
    k_ref[...] = k.astype(BF16)
    v = p[:, V_START:]
    for t in range(TM // KT):
        vt_ref[t] = v[t * KT:(t + 1) * KT, :].T.astype(BF16)


def _inproj(x, mod_l, g, w, mch, ones, qg, kg, cos, slo, shi):
    n_tiles = N_TOK // TM
    n_lat_tiles = N_LAT // TM
    pos_tiles = SEQ // TM

    def pos_map(i):
        return (jnp.where(i < n_lat_tiles, i % pos_tiles, pos_tiles), 0)

    row = lambda i: (i, 0)
    pos_spec = pl.BlockSpec((TM, LANES), pos_map)
    return pl.pallas_call(
        _inproj_kernel,
        grid=(n_tiles,),
        in_specs=[pl.BlockSpec((TM, D_MODEL), row),
                  pl.BlockSpec((None, N_MOD, D_MODEL), lambda i: (_mod_row(i), 0, 0)),
                  _full((1, D_MODEL)),
                  _full((D_MODEL, PROJ_W)),
                  _full((FOUR_W, 2 * FOUR_W)),
                  _full((ATT_W, ATT_W)),
                  _full((1, ATT_W)),
                  _full((1, KV_W)),
                  pos_spec, pos_spec, pos_spec],
        out_specs=[pl.BlockSpec((TM, 3 * CONV_W), row),
                   pl.BlockSpec((TM, FOUR_W), row),
                   pl.BlockSpec((TM, FOUR_W), row),
                   pl.BlockSpec((ATT_W, TM), lambda i: (0, i)),
                   pl.BlockSpec((TM, KV_W), row),
                   pl.BlockSpec((TM // KT, KV_W, KT), lambda i: (i, 0, 0))],
        out_shape=[jax.ShapeDtypeStruct((N_TOK, 3 * CONV_W), BF16),
                   jax.ShapeDtypeStruct((N_TOK, FOUR_W), BF16),
                   jax.ShapeDtypeStruct((N_TOK, FOUR_W), BF16),
                   jax.ShapeDtypeStruct((ATT_W, N_TOK), BF16),
                   jax.ShapeDtypeStruct((N_TOK, KV_W), BF16),
                   jax.ShapeDtypeStruct((N_TOK // KT, KV_W, KT), BF16)],
        compiler_params=_params(1),
        name="in_proj",
    )(x, mod_l, g, w, mch, ones, qg, kg, cos, slo, shi)


def _attn_body(qt_ref, kc_ref, vtc_ref, kl_ref, vtl_ref, o_ref, s_ref, ot_ref, *, n_lat_tiles):
    sub = 8

    def score_tile(w, k_tile, buf, row0, macc):
        s = jnp.dot(k_tile, w, preferred_element_type=F32)
        s_ref[buf, pl.ds(row0, KT), :] = s
        return jnp.maximum(macc, jnp.max(s.reshape(KT // sub, sub, TQ), axis=0))

    def prob_tile(m, vt_tile, buf, row0, lacc, acc):
        p = jnp.exp2(s_ref[buf, pl.ds(row0, KT), :] - m)
        lacc = lacc + jnp.sum(p.reshape(KT // sub, sub, TQ), axis=0)
        acc = acc + jnp.dot(vt_tile, p.astype(BF16), preferred_element_type=F32)
        return lacc, acc

    m_prev = None
    for stage in range(N_HEADS + 1):
        scoring, weighting = stage < N_HEADS, stage >= 1
        buf_a, buf_b = stage % 2, (stage - 1) % 2
        macc = jnp.full((sub, TQ), -jnp.inf, F32)
        lacc = jnp.zeros((sub, TQ), F32)
        acc = jnp.zeros((HEAD_DIM, TQ), F32)
        if scoring:
            qt = qt_ref[stage * HEAD_DIM:(stage + 1) * HEAD_DIM, :]
            zero = jnp.zeros_like(qt)
            w = jnp.concatenate([qt, zero] if stage // GROUP == 0 else [zero, qt], axis=0)
            macc = score_tile(w, kc_ref[...], buf_a, 0, macc)
        if weighting:
            hv = (stage - 1) // GROUP
            v_rows = slice(hv * HEAD_DIM, (hv + 1) * HEAD_DIM)
            lacc, acc = prob_tile(m_prev, vtc_ref[0, v_rows, :], buf_b, 0, lacc, acc)
        for j in range(n_lat_tiles):
            row0 = CTX_LEN + j * KT
            if scoring:
                macc = score_tile(w, kl_ref[j * KT:(j + 1) * KT, :], buf_a, row0, macc)
            if weighting:
                lacc, acc = prob_tile(m_prev, vtl_ref[j, v_rows, :], buf_b, row0, lacc, acc)
        if weighting:
            g = stage - 1
            ot_ref[g * HEAD_DIM:(g + 1) * HEAD_DIM, :] = acc / jnp.sum(lacc, axis=0, keepdims=True)
        if scoring:
            m_prev = jnp.max(macc, axis=0, keepdims=True)
    o_ref[...] = ot_ref[...].T.astype(o_ref.dtype)


def _attn_lat_kernel(qt_ref, kc_ref, vtc_ref, kl_ref, vtl_ref, o_ref, s_ref, ot_ref):
    _attn_body(qt_ref, kc_ref, vtc_ref, kl_ref, vtl_ref, o_ref, s_ref, ot_ref, n_lat_tiles=SEQ // KT)


def _attn_ctx_kernel(qt_ref, kc_ref, vtc_ref, prev_ref, o_ref, s_ref, ot_ref):
    del prev_ref
    _attn_body(qt_ref, kc_ref, vtc_ref, None, None, o_ref, s_ref, ot_ref, n_lat_tiles=0)


def _attn_scratch(n_keys):
    return [pltpu.VMEM((2, n_keys, TQ), F32), pltpu.VMEM((ATT_W, TQ), F32)]


def _attn_lat(qt, k, vt):
    nq = SEQ // TQ
    ctx0 = N_LAT // CTX_LEN
    return pl.pallas_call(
        _attn_lat_kernel,
        grid=(BATCH, nq),
        in_specs=[pl.BlockSpec((ATT_W, TQ), lambda b, j: (0, b * nq + j)),
                  pl.BlockSpec((CTX_LEN, KV_W), lambda b, j: (ctx0 + b, 0)),
                  pl.BlockSpec((CTX_LEN // KT, KV_W, KT), lambda b, j: (ctx0 + b, 0, 0)),
                  pl.BlockSpec((SEQ, KV_W), lambda b, j: (b, 0)),
                  pl.BlockSpec((SEQ // KT, KV_W, KT), lambda b, j: (b, 0, 0))],
        out_specs=pl.BlockSpec((TQ, ATT_W), lambda b, j: (b * nq + j, 0)),
        out_shape=jax.ShapeDtypeStruct((N_TOK, ATT_W), BF16),
        scratch_shapes=_attn_scratch(CTX_LEN + SEQ),
        compiler_params=_params(2),
        name="attn_latent",
    )(qt, k, vt, k, vt)


def _attn_ctx(qt, k, vt, att):
    ctx0 = N_LAT // CTX_LEN
    return pl.pallas_call(
        _attn_ctx_kernel,
        grid=(BATCH,),
        in_specs=[pl.BlockSpec((ATT_W, TQ), lambda b: (0, ctx0 + b)),
                  pl.BlockSpec((CTX_LEN, KV_W), lambda b: (ctx0 + b, 0)),
                  pl.BlockSpec((CTX_LEN // KT, KV_W, KT), lambda b: (ctx0 + b, 0, 0)),
                  pl.BlockSpec(memory_space=pl.ANY)],
        out_specs=pl.BlockSpec((TQ, ATT_W), lambda b: (ctx0 + b, 0)),
        out_shape=jax.ShapeDtypeStruct((N_TOK, ATT_W), BF16),
        input_output_aliases={3: 0},
        scratch_shapes=_attn_scratch(CTX_LEN),
        compiler_params=_params(1),
        name="attn_context",
    )(qt, k, vt, att)


def _four1_kernel(zre_ref, zim_ref, c_ref, s_ref, are_ref, aim_ref):
    zre, zim, c, s = zre_ref[...], zim_ref[...], c_ref[...], s_ref[...]
    dot = functools.partial(jnp.dot, preferred_element_type=F32)
    are_ref[...] = (dot(c, zre) + dot(s, zim)).astype(BF16)
    aim_ref[...] = (dot(c, zim) - dot(s, zre)).astype(BF16)


def _four1(zre, zim, c64, s64):
    cols = DFT_R * FOUR_W
    zre_v = zre.reshape(N_TOK // DFT_R, cols)
    zim_v = zim.reshape(N_TOK // DFT_R, cols)
    blk = pl.BlockSpec((DFT_R, F1_COLS), lambda b, j: (b, j))
    out = jax.ShapeDtypeStruct((BATCH * DFT_R, cols), BF16)
    return pl.pallas_call(
        _four1_kernel,
        grid=(BATCH, cols // F1_COLS),
        in_specs=[blk, blk, _full((DFT_R, DFT_R)), _full((DFT_R, DFT_R))],
        out_specs=[blk, blk],
        out_shape=[out, out],
        compiler_params=_params(2),
        name="fourier_stage1",
    )(zre_v, zim_v, c64, s64)


def _four2_kernel(are_ref, aim_ref, tc_ref, ts_ref, o_ref):
    y = (jnp.dot(tc_ref[...], are_ref[...], preferred_element_type=F32)
         + jnp.dot(ts_ref[...], aim_ref[...], preferred_element_type=F32))
    o_ref[...] = y.reshape(DFT_R, F2_GROUP, FOUR_W)


def _four2(are, aim, tc, ts):
    n_groups = DFT_R // F2_GROUP
    rows = F2_GROUP * DFT_R
    are_v = are.reshape(N_LAT, FOUR_W)
    aim_v = aim.reshape(N_LAT, FOUR_W)
    a_spec = pl.BlockSpec((rows, FOUR_W), lambda m, b: (b * n_groups + m, 0))
    t_spec = pl.BlockSpec((None, rows, rows), lambda m, b: (m, 0, 0))
    yf = pl.pallas_call(
        _four2_kernel,
        grid=(n_groups, BATCH),
        in_specs=[a_spec, a_spec, t_spec, t_spec],
        out_specs=pl.BlockSpec((DFT_R, None, F2_GROUP, FOUR_W), lambda m, b: (b, m, 0, 0)),
        out_shape=jax.ShapeDtypeStruct((N_TOK // DFT_R, n_groups, F2_GROUP, FOUR_W), F32),
        compiler_params=_params(2),
        name="fourier_stage2",
    )(are_v, aim_v, tc, ts)
    return yf.reshape(N_TOK, FOUR_W)


def _four_ctx_kernel(zre_ref, zim_ref, c_ref, s_ref, prev_ref, o_ref):
    del prev_ref
    o_ref[...] = (jnp.dot(c_ref[...], zre_ref[...], preferred_element_type=F32)
                  + jnp.dot(s_ref[...], zim_ref[...], preferred_element_type=F32))


def _four_ctx(zre, zim, c256, s256, yf):
    ctx0 = N_LAT // CTX_LEN
    blk = pl.BlockSpec((CTX_LEN, FOUR_W), lambda b: (ctx0 + b, 0))
    return pl.pallas_call(
        _four_ctx_kernel,
        grid=(BATCH,),
        in_specs=[blk, blk, _full((CTX_LEN, CTX_LEN)), _full((CTX_LEN, CTX_LEN)),
                  pl.BlockSpec(memory_space=pl.ANY)],
        out_specs=blk,
        out_shape=jax.ShapeDtypeStruct((N_TOK, FOUR_W), F32),
        input_output_aliases={4: 0},
        compiler_params=_params(1),
        name="fourier_context",
    )(zre, zim, c256, s256, yf)


def _outproj_kernel(ubc_ref, hp_ref, hn_ref, yf_ref, att_ref, x_ref, mod_ref, g_ref, cw_ref, wo_ref, o_ref):
    i = pl.program_id(0)
    ubc = ubc_ref[...].astype(F32)
    u, bg, cg = ubc[:, :CONV_W], ubc[:, CONV_W:2 * CONV_W], ubc[:, 2 * CONV_W:]
    z = cg * u
    hp = hp_ref[HALO - 1:HALO, :].astype(F32)
    hn = hn_ref[0:1, :].astype(F32)
    z_before = hp[:, 2 * CONV_W:] * hp[:, :CONV_W]
    z_after = hn[:, 2 * CONV_W:] * hn[:, :CONV_W]
    row = lax.broadcasted_iota(jnp.int32, (TM, 1), 0)
    z_prev = jnp.where(row == 0, z_before, pltpu.roll(z, 1, 0))
    z_next = jnp.where(row == TM - 1, z_after, pltpu.roll(z, TM - 1, 0))
    tok = i * TM + row
    seg = jnp.where(tok < N_LAT, SEQ, CTX_LEN)
    pos = jnp.bitwise_and(tok, seg - 1)
    z_prev = jnp.where(pos == 0, 0.0, z_prev)
    z_next = jnp.where(pos == seg - 1, 0.0, z_next)
    conv = bg * (z_prev * cw_ref[0:1, :] + z * cw_ref[1:2, :] + z_next * cw_ref[2:3, :])
    dot = functools.partial(jnp.dot, preferred_element_type=F32)
    y = (dot(conv.astype(BF16), wo_ref[:CONV_W, :])
         + dot(yf_ref[...].astype(BF16), wo_ref[CONV_W:CONV_W + FOUR_W, :])
         + dot(att_ref[...], wo_ref[CONV_W + FOUR_W:, :]))
    o_ref[...] = x_ref[...] + mod_ref[2:3, :] * _rms(y, g_ref[...])


def _outproj(n_rows, ubc, yf, att, x, mod_l, g, cw, wo):
    n_tiles = n_rows // TM
    per = TM // HALO
    last_halo = n_rows // HALO - 1
    row = lambda i: (i, 0)
    return pl.pallas_call(
        _outproj_kernel,
        grid=(n_tiles,),
        in_specs=[pl.BlockSpec((TM, 3 * CONV_W), row),
                  pl.BlockSpec((HALO, 3 * CONV_W), lambda i: (jnp.maximum(i * per - 1, 0), 0)),
                  pl.BlockSpec((HALO, 3 * CONV_W), lambda i: (jnp.minimum((i + 1) * per, last_halo), 0)),
                  pl.BlockSpec((TM, FOUR_W), row),
                  pl.BlockSpec((TM, ATT_W), row),
                  pl.BlockSpec((TM, D_MODEL), row),
                  pl.BlockSpec((None, N_MOD, D_MODEL), lambda i: (_mod_row(i), 0, 0)),
                  _full((1, D_MODEL)),
                  _full((3, CONV_W)),
                  _full((MIX_W_ROWS, D_MODEL))],
        out_specs=pl.BlockSpec((TM, D_MODEL), row),
        out_shape=jax.ShapeDtypeStruct((n_rows, D_MODEL), F32),
        compiler_params=_params(1),
        name="out_proj",
    )(ubc, ubc, ubc, yf, att, x, mod_l, g, cw, wo)


FF_CHUNK = 1024


def _ffn_kernel(x_ref, mod_ref, gpre_ref, gpost_ref, wu_ref, wd_ref, o_ref):
    x = x_ref[...]
    h = (_rms(x, gpre_ref[...]) * (1.0 + mod_ref[4:5, :]) + mod_ref[3:4, :]).astype(BF16)
    acc = jnp.zeros((TM, D_MODEL), F32)
    for c in range(D_FF // FF_CHUNK):
        cs = slice(c * FF_CHUNK, (c + 1) * FF_CHUNK)
        a = jnp.maximum(jnp.dot(h, wu_ref[:, cs], preferred_element_type=F32), 0.0)
        acc = acc + jnp.dot((a * a).astype(BF16), wd_ref[cs, :], preferred_element_type=F32)
    o_ref[...] = x + mod_ref[5:6, :] * _rms(acc, gpost_ref[...])


def _ffn(n_rows, x, mod_l, gpre, gpost, wu, wd):
    row = lambda i: (i, 0)
    return pl.pallas_call(
        _ffn_kernel,
        grid=(n_rows // TM,),
        in_specs=[pl.BlockSpec((TM, D_MODEL), row),
                  pl.BlockSpec((None, N_MOD, D_MODEL), lambda i: (_mod_row(i), 0, 0)),
                  _full((1, D_MODEL)),
                  _full((1, D_MODEL)),
                  _full((D_MODEL, D_FF)),
                  _full((D_FF, D_MODEL))],
        out_specs=pl.BlockSpec((TM, D_MODEL), row),
        out_shape=jax.ShapeDtypeStruct((n_rows, D_MODEL), F32),
        compiler_params=_params(1),
        name="ffn",
    )(x, mod_l, gpre, gpost, wu, wd)


def _rope_tables():
    t = jnp.arange(SEQ)
    rowp = (t // GRID_W).astype(F32)
    colp = (t % GRID_W).astype(F32)
    inv = 1.0 / (ROPE_THETA ** (jnp.arange(0, AXIS_DIM, 2, dtype=F32) / AXIS_DIM))
    ang = jnp.stack([rowp[:, None] * inv, colp[:, None] * inv], axis=1)
    ang = jnp.broadcast_to(ang[:, :, None, :], (SEQ, 2, 2, AXIS_DIM // 2)).reshape(SEQ, HEAD_DIM)
    cos, sin = jnp.cos(ang), jnp.sin(ang)
    first_half = (jnp.arange(HEAD_DIM) % AXIS_DIM) < AXIS_DIM // 2
    sin_lo = jnp.where(first_half, -sin, 0.0)
    sin_hi = jnp.where(first_half, 0.0, sin)
    ident = lambda v: jnp.full((TM, HEAD_DIM), v, F32)
    wide = lambda a, v: jnp.tile(jnp.concatenate([a, ident(v)], axis=0), (1, LANES // HEAD_DIM))
    return wide(cos, 1.0), wide(sin_lo, 0.0), wide(sin_hi, 0.0)


def _dft_tables():
    two_pi = 2.0 * np.pi
    n = np.arange(FOUR_GROUP_DIM)
    ang = two_pi * np.outer(n, n) / FOUR_GROUP_DIM
    eye = np.eye(FOUR_GROUPS)
    scale = FOUR_GROUP_DIM ** -0.5
    mch = np.concatenate([np.kron(eye, np.cos(ang)), -np.kron(eye, np.sin(ang))], axis=1) * scale
    r = np.arange(DFT_R)
    ang_r = two_pi * np.outer(r, r) / DFT_R
    c64, s64 = np.cos(ang_r), np.sin(ang_r)
    n_groups = DFT_R // F2_GROUP
    m = np.arange(n_groups)[:, None, None, None]
    k2 = np.arange(DFT_R)[None, :, None, None]
    g = np.arange(F2_GROUP)[None, None, :, None]
    c = np.arange(DFT_R)[None, None, None, :]
    k = F2_GROUP * m + g + DFT_R * k2
    ang2 = two_pi * ((k * c) % SEQ) / SEQ
    sel = np.eye(F2_GROUP)[None, None, :, :, None]
    norm2 = SEQ ** -0.5
    tc = (np.cos(ang2)[:, :, :, None, :] * sel * norm2).reshape(n_groups, DFT_R * F2_GROUP, F2_GROUP * DFT_R)
    ts = (np.sin(ang2)[:, :, :, None, :] * sel * norm2).reshape(n_groups, DFT_R * F2_GROUP, F2_GROUP * DFT_R)
    t = np.arange(CTX_LEN)
    ang_c = two_pi * (np.outer(t, t) % CTX_LEN) / CTX_LEN
    normc = CTX_LEN ** -0.5
    c256, s256 = np.cos(ang_c) * normc, np.sin(ang_c) * normc
    as_bf16 = lambda a: jnp.asarray(a, dtype=F32).astype(BF16)
    return tuple(as_bf16(a) for a in (mch, c64, s64, tc, ts, c256, s256))


def _head_mean_matrix():
    blocks = np.kron(np.eye(N_HEADS), np.full((HEAD_DIM, HEAD_DIM), 1.0 / HEAD_DIM))
    return jnp.asarray(blocks, dtype=F32).astype(BF16)


def kernel(x, c, ctx, c_ctx, w_ada, b_ada, w_in, conv_w, q_gain, k_gain, w_out,
           g_pre_mix, g_post_mix, g_pre_ffn, g_post_ffn, w_ffn_up, w_ffn_down):
    cos, sin_lo, sin_hi = _rope_tables()
    mch, c64, s64, tc, ts, c256, s256 = _dft_tables()
    ones = _head_mean_matrix()

    c8 = jnp.concatenate([c, c_ctx[None, :], jnp.zeros((MOD_ROWS - BATCH - 1, D_MODEL), F32)], axis=0)
    mod = _ada(c8, w_ada, b_ada).reshape(DEPTH, MOD_ROWS, N_MOD, D_MODEL)

    xs = jnp.concatenate([x.reshape(N_LAT, D_MODEL), ctx.reshape(N_CTX, D_MODEL)], axis=0)
    w_in_b, w_out_b = w_in.astype(BF16), w_out.astype(BF16)
    w_up_b, w_down_b = w_ffn_up.astype(BF16), w_ffn_down.astype(BF16)
    vec = lambda a: a.reshape(1, -1)

    for l in range(DEPTH):
        last = l == DEPTH - 1
        qg = vec(jnp.tile(q_gain[l], N_HEADS))
        kg = vec(jnp.tile(k_gain[l], N_KV))
        ubc, zre, zim, qt, k, vt = _inproj(xs, mod[l], vec(g_pre_mix[l]), w_in_b[l], mch, ones,
                                           qg, kg, cos, sin_lo, sin_hi)
        att = _attn_lat(qt, k, vt)
        are, aim = _four1(zre, zim, c64, s64)
        yf = _four2(are, aim, tc, ts)
        if last:
            n_rows = N_LAT
        else:
            n_rows = N_TOK
            att = _attn_ctx(qt, k, vt, att)
            yf = _four_ctx(zre, zim, c256, s256, yf)
        xs = _outproj(n_rows, ubc, yf, att, xs, mod[l], vec(g_post_mix[l]), conv_w[l], w_out_b[l])
        xs = _ffn(n_rows, xs, mod[l], vec(g_pre_ffn[l]), vec(g_post_ffn[l]), w_up_b[l], w_down_b[l])
    return xs.reshape(BATCH, SEQ, D_MODEL)
```

```python
import functools
import math

import jax
import jax.numpy as jnp
import numpy as np
from jax import lax
from jax.experimental import pallas as pl
from jax.experimental.pallas import tpu as pltpu

F32 = jnp.float32
BF16 = jnp.bfloat16

D_MODEL = 1024
BATCH = 4
SEQ = 4096
DEPTH = 4
GRID_W = 64
CTX_LEN = 256
CONV_W = 256
FOUR_GROUPS = 4
FOUR_GROUP_DIM = 64
FOUR_W = FOUR_GROUPS * FOUR_GROUP_DIM
N_HEADS = 8
N_KV = 2
GROUP = N_HEADS // N_KV
HEAD_DIM = 64
ATT_W = N_HEADS * HEAD_DIM
KV_W = N_KV * HEAD_DIM
Q_START = 3 * CONV_W + FOUR_W
K_START = Q_START + ATT_W
V_START = K_START + KV_W
PROJ_W = V_START + KV_W
AXIS_DIM = HEAD_DIM // 2
ROPE_THETA = 10000.0
D_FF = 4 * D_MODEL
N_MOD = 6
EPS = 1e-6

N_LAT = BATCH * SEQ
N_CTX = BATCH * CTX_LEN
N_TOK = N_LAT + N_CTX
MOD_ROWS = 8
CTX_MOD_ROW = BATCH

LANES = 128
BF16_SUBLANES = 16
VMEM_LIMIT_BYTES = 56 * 1024 * 1024

TM = 1024
SUB = 256
TM_FF = 512
TQ = 256
KT = 256
SCORE_LEAD = 6
LOG2_E = math.log2(math.e)
MIX_W_ROWS = CONV_W + FOUR_W + ATT_W
DFT_R = 64
F2_GROUP = 8
HALO = BF16_SUBLANES

assert SEQ == DFT_R * DFT_R and SEQ % TM == 0 and N_CTX % TM == 0 and SEQ % TM_FF == 0 and N_CTX % TM_FF == 0
assert CTX_LEN == TQ == KT == SUB and SEQ % TQ == 0 and TM % SUB == 0


def _params(n_axes, flags=None):
    return pltpu.CompilerParams(dimension_semantics=("arbitrary",) * n_axes,
                                vmem_limit_bytes=VMEM_LIMIT_BYTES, flags=flags)


def _full(shape):
    return pl.BlockSpec(shape, lambda *_: (0,) * len(shape))


def _mod_spec(tm):
    def index(i):
        return (jnp.where(i < N_LAT // tm, i // (SEQ // tm), CTX_MOD_ROW), 0, 0)
    return pl.BlockSpec((None, N_MOD, D_MODEL), index)


def _ada_kernel(c_ref, w_ref, b_ref, o_ref):
    c = c_ref[...]
    s = c * jax.nn.sigmoid(c)
    o_ref[...] = jnp.dot(s, w_ref[...], preferred_element_type=F32,
                         precision=lax.Precision.HIGHEST) + b_ref[...]


def _ada(c8, w_ada, b_ada):
    tn = 1024
    n = N_MOD * D_MODEL
    return pl.pallas_call(
        _ada_kernel,
        grid=(DEPTH, n // tn),
        in_specs=[pl.BlockSpec((MOD_ROWS, D_MODEL), lambda l, j: (0, 0)),
                  pl.BlockSpec((None, D_MODEL, tn), lambda l, j: (l, 0, j)),
                  pl.BlockSpec((None, 1, tn), lambda l, j: (l, 0, j))],
        out_specs=pl.BlockSpec((None, MOD_ROWS, tn), lambda l, j: (l, 0, j)),
        out_shape=jax.ShapeDtypeStruct((DEPTH, MOD_ROWS, n), F32),
        compiler_params=_params(2),
        name="ada",
    )(c8, w_ada, b_ada.reshape(DEPTH, 1, n))


def _rms(x, g):
    return x * lax.rsqrt(jnp.mean(x * x, axis=-1, keepdims=True) + EPS) * g


def _head_norm_rope(t, ones, gain, cos, sin_lo, sin_hi):
    w = t.shape[-1]
    ms = jnp.dot((t * t).astype(BF16), ones, preferred_element_type=F32)
    tn = t * lax.rsqrt(ms + EPS) * gain
    up = pltpu.roll(tn, w - AXIS_DIM // 2, 1)
    dn = pltpu.roll(tn, AXIS_DIM // 2, 1)
    return tn * cos + up * sin_lo + dn * sin_hi


def _inproj_kernel(x_ref, mod_ref, g_ref, w_ref, mch_ref, ones_ref, qg_ref, kg_ref,
                   cos_ref, slo_ref, shi_ref,
                   ubc_ref, zre_ref, zim_ref, qt_ref, k_ref, vt_ref):
    rep = ATT_W // LANES
    for sb in range(TM // SUB):
        rows = slice(sb * SUB, (sb + 1) * SUB)
        h = _rms(x_ref[rows, :], g_ref[...]) * (1.0 + mod_ref[1:2, :]) + mod_ref[0:1, :]
        p = jnp.dot(h.astype(BF16), w_ref[...], preferred_element_type=F32)
        ubc_ref[rows, :] = p[:, :3 * CONV_W].astype(BF16)
        z = jnp.dot(p[:, 3 * CONV_W:Q_START].astype(BF16), mch_ref[...], preferred_element_type=F32)
        zre_ref[rows, :] = z[:, :FOUR_W]
        zim_ref[rows, :] = z[:, FOUR_W:]
        cos, slo, shi = cos_ref[rows, :], slo_ref[rows, :], shi_ref[rows, :]
        q = _head_norm_rope(p[:, Q_START:K_START], ones_ref[...], qg_ref[...],
                            jnp.tile(cos, (1, rep)), jnp.tile(slo, (1, rep)), jnp.tile(shi, (1, rep)))
        qt_ref[:, rows] = (q * (HEAD_DIM ** -0.5 * LOG2_E)).T.astype(BF16)
        k = _head_norm_rope(p[:, K_START:V_START], ones_ref[:KV_W, :KV_W], kg_ref[...], cos, slo, shi)
        k_ref[rows, :] = k.astype(BF16)
        vt_ref[sb] = p[:, V_START:].T.astype(BF16)


def _inproj(x, mod_l, g, w, mch, ones, qg, kg, cos, slo, shi):
    n_tiles = N_TOK // TM
    n_lat_tiles = N_LAT // TM
    pos_tiles = SEQ // TM

    def pos_map(i):
        return (jnp.where(i < n_lat_tiles, i % pos_tiles, pos_tiles), 0)

    row = lambda i: (i, 0)
    pos_spec = pl.BlockSpec((TM, LANES), pos_map)
    return pl.pallas_call(
        _inproj_kernel,
        grid=(n_tiles,),
        in_specs=[pl.BlockSpec((TM, D_MODEL), row),
                  _mod_spec(TM),
                  _full((1, D_MODEL)),
                  _full((D_MODEL, PROJ_W)),
                  _full((FOUR_W, 2 * FOUR_W)),
                  _full((ATT_W, ATT_W)),
                  _full((1, ATT_W)),
                  _full((1, KV_W)),
                  pos_spec, pos_spec, pos_spec],
        out_specs=[pl.BlockSpec((TM, 3 * CONV_W), row),
                   pl.BlockSpec((TM, FOUR_W), row),
                   pl.BlockSpec((TM, FOUR_W), row),
                   pl.BlockSpec((ATT_W, TM), lambda i: (0, i)),
                   pl.BlockSpec((TM, KV_W), row),
                   pl.BlockSpec((TM // KT, KV_W, KT), lambda i: (i, 0, 0))],
        out_shape=[jax.ShapeDtypeStruct((N_TOK, 3 * CONV_W), BF16),
                   jax.ShapeDtypeStruct((N_TOK, FOUR_W), F32),
                   jax.ShapeDtypeStruct((N_TOK, FOUR_W), F32),
                   jax.ShapeDtypeStruct((ATT_W, N_TOK), BF16),
                   jax.ShapeDtypeStruct((N_TOK, KV_W), BF16),
                   jax.ShapeDtypeStruct((N_TOK // KT, KV_W, KT), BF16)],
        compiler_params=_params(1),
        name="in_proj",
    )(x, mod_l, g, w, mch, ones, qg, kg, cos, slo, shi)


def _attn_body(qt_ref, kc_ref, vtc_ref, kl_ref, vtl_ref, o_ref, ot_ref, *, n_lat_tiles):
    sub = 8
    ones = jnp.ones((BF16_SUBLANES, KT), BF16)

    n_tiles = 1 + n_lat_tiles
    work = [(g, j) for g in range(N_HEADS) for j in range(n_tiles)]

    def scores(g, j):
        qt = qt_ref[g * HEAD_DIM:(g + 1) * HEAD_DIM, :]
        zero = jnp.zeros_like(qt)
        w = jnp.concatenate([qt, zero] if g // GROUP == 0 else [zero, qt], axis=0)
        k_tile = kc_ref[...] if j == 0 else kl_ref[(j - 1) * KT:j * KT, :]
        return jnp.dot(k_tile, w, preferred_element_type=F32)

    def accumulate(g, j, s, m, acc):
        hv = g // GROUP
        v_rows = slice(hv * HEAD_DIM, (hv + 1) * HEAD_DIM)
        vt_tile = vtc_ref[0, v_rows, :] if j == 0 else vtl_ref[j - 1, v_rows, :]
        tile_max = jnp.max(jnp.max(s.reshape(KT // sub, sub, TQ), axis=0), axis=0, keepdims=True)
        m_new = tile_max if j == 0 else jnp.maximum(m, tile_max)
        p = jnp.exp2(s - m_new).astype(BF16)
        lhs = jnp.concatenate([vt_tile, ones], axis=0)
        pv = jnp.dot(lhs, p, preferred_element_type=F32)
        if j > 0:
            pv = pv + acc * jnp.exp2(m - m_new)
        return m_new, pv

    pending = {}
    m = acc = None
    for step in range(len(work) + SCORE_LEAD):
        if step < len(work):
            pending[step] = scores(*work[step])
        if step >= SCORE_LEAD:
            g, j = work[step - SCORE_LEAD]
            m, acc = accumulate(g, j, pending.pop(step - SCORE_LEAD), m, acc)
            if j == n_tiles - 1:
                ot_ref[g * HEAD_DIM:(g + 1) * HEAD_DIM, :] = acc[:HEAD_DIM] / acc[HEAD_DIM:HEAD_DIM + 1]
    o_ref[...] = ot_ref[...].T.astype(o_ref.dtype)


def _attn_lat_kernel(qt_ref, kc_ref, vtc_ref, kl_ref, vtl_ref, o_ref, ot_ref):
    _attn_body(qt_ref, kc_ref, vtc_ref, kl_ref, vtl_ref, o_ref, ot_ref, n_lat_tiles=SEQ // KT)


def _attn_ctx_kernel(qt_ref, kc_ref, vtc_ref, prev_ref, o_ref, ot_ref):
    del prev_ref
    _attn_body(qt_ref, kc_ref, vtc_ref, None, None, o_ref, ot_ref, n_lat_tiles=0)


ATTN_FLAGS = None


def _attn_scratch():
    return [pltpu.VMEM((ATT_W, TQ), F32)]


def _attn_lat(qt, k, vt):
    nq = SEQ // TQ
    ctx0 = N_LAT // CTX_LEN
    return pl.pallas_call(
        _attn_lat_kernel,
        grid=(BATCH, nq),
        in_specs=[pl.BlockSpec((ATT_W, TQ), lambda b, j: (0, b * nq + j)),
                  pl.BlockSpec((CTX_LEN, KV_W), lambda b, j: (ctx0 + b, 0)),
                  pl.BlockSpec((CTX_LEN // KT, KV_W, KT), lambda b, j: (ctx0 + b, 0, 0)),
                  pl.BlockSpec((SEQ, KV_W), lambda b, j: (b, 0)),
                  pl.BlockSpec((SEQ // KT, KV_W, KT), lambda b, j: (b, 0, 0))],
        out_specs=pl.BlockSpec((TQ, ATT_W), lambda b, j: (b * nq + j, 0)),
        out_shape=jax.ShapeDtypeStruct((N_TOK, ATT_W), BF16),
        scratch_shapes=_attn_scratch(),
        compiler_params=_params(2, ATTN_FLAGS),
        name="attn_latent",
    )(qt, k, vt, k, vt)


def _attn_ctx(qt, k, vt, att):
    ctx0 = N_LAT // CTX_LEN
    return pl.pallas_call(
        _attn_ctx_kernel,
        grid=(BATCH,),
        in_specs=[pl.BlockSpec((ATT_W, TQ), lambda b: (0, ctx0 + b)),
                  pl.BlockSpec((CTX_LEN, KV_W), lambda b: (ctx0 + b, 0)),
                  pl.BlockSpec((CTX_LEN // KT, KV_W, KT), lambda b: (ctx0 + b, 0, 0)),
                  pl.BlockSpec(memory_space=pl.ANY)],
        out_specs=pl.BlockSpec((TQ, ATT_W), lambda b: (ctx0 + b, 0)),
        out_shape=jax.ShapeDtypeStruct((N_TOK, ATT_W), BF16),
        input_output_aliases={3: 0},
        scratch_shapes=_attn_scratch(),
        compiler_params=_params(1),
        name="attn_context",
    )(qt, k, vt, att)


def _four_lat_kernel(zre0_ref, zre1_ref, zim0_ref, zim1_ref, cs_ref, tc_ref, ts_ref, o_ref,
                     are0_ref, are1_ref, aim0_ref, aim1_ref):
    cs = cs_ref[...]
    both = lambda lo_ref, hi_ref, idx: jnp.concatenate([lo_ref[idx, :], hi_ref[idx, :]], axis=1)
    for c in range(DFT_R):
        strided = pl.ds(c, DFT_R, stride=DFT_R)
        pr = jnp.dot(cs, both(zre0_ref, zre1_ref, strided).astype(BF16), preferred_element_type=F32)
        pi = jnp.dot(cs, both(zim0_ref, zim1_ref, strided).astype(BF16), preferred_element_type=F32)
        are = pr[:DFT_R] + pi[DFT_R:]
        aim = pi[:DFT_R] - pr[DFT_R:]
        are0_ref[strided, :] = are[:, :LANES]
        are1_ref[strided, :] = are[:, LANES:]
        aim0_ref[strided, :] = aim[:, :LANES]
        aim1_ref[strided, :] = aim[:, LANES:]
    rows = F2_GROUP * DFT_R
    for m in range(DFT_R // F2_GROUP):
        blk = slice(m * rows, (m + 1) * rows)
        y = (jnp.dot(tc_ref[m], both(are0_ref, are1_ref, blk).astype(BF16), preferred_element_type=F32)
             + jnp.dot(ts_ref[m], both(aim0_ref, aim1_ref, blk).astype(BF16), preferred_element_type=F32))
        o_ref[:, m, :, :] = y.reshape(DFT_R, F2_GROUP, FOUR_W)


def _four_lat(zre, zim, cs64, tc, ts):
    n_groups = DFT_R // F2_GROUP
    rows = F2_GROUP * DFT_R
    lo = pl.BlockSpec((SEQ, LANES), lambda b: (b, 0))
    hi = pl.BlockSpec((SEQ, LANES), lambda b: (b, 1))
    t_spec = pl.BlockSpec((n_groups, rows, rows), lambda b: (0, 0, 0), pipeline_mode=pl.Buffered(1))
    yf = pl.pallas_call(
        _four_lat_kernel,
        grid=(BATCH,),
        in_specs=[lo, hi, lo, hi, _full((2 * DFT_R, DFT_R)), t_spec, t_spec],
        out_specs=pl.BlockSpec((DFT_R, n_groups, F2_GROUP, FOUR_W), lambda b: (b, 0, 0, 0)),
        out_shape=jax.ShapeDtypeStruct((N_TOK // DFT_R, n_groups, F2_GROUP, FOUR_W), F32),
        scratch_shapes=[pltpu.VMEM((SEQ, LANES), F32)] * 4,
        compiler_params=_params(1),
        name="fourier_latent",
    )(zre, zre, zim, zim, cs64, tc, ts)
    return yf.reshape(N_TOK, FOUR_W)


def _four_ctx_kernel(zre_ref, zim_ref, c_ref, s_ref, prev_ref, o_ref):
    del prev_ref
    o_ref[...] = (jnp.dot(c_ref[...], zre_ref[...].astype(BF16), preferred_element_type=F32)
                  + jnp.dot(s_ref[...], zim_ref[...].astype(BF16), preferred_element_type=F32))


def _four_ctx(zre, zim, c256, s256, yf):
    ctx0 = N_LAT // CTX_LEN
    blk = pl.BlockSpec((CTX_LEN, FOUR_W), lambda b: (ctx0 + b, 0))
    return pl.pallas_call(
        _four_ctx_kernel,
        grid=(BATCH,),
        in_specs=[blk, blk, _full((CTX_LEN, CTX_LEN)), _full((CTX_LEN, CTX_LEN)),
                  pl.BlockSpec(memory_space=pl.ANY)],
        out_specs=blk,
        out_shape=jax.ShapeDtypeStruct((N_TOK, FOUR_W), F32),
        input_output_aliases={4: 0},
        compiler_params=_params(1),
        name="fourier_context",
    )(zre, zim, c256, s256, yf)


def _outproj_kernel(ubc_ref, hp_ref, hn_ref, yf_ref, att_ref, x_ref, mod_ref, g_ref, cw_ref, wo_ref, o_ref):
    i = pl.program_id(0)
    ubc = ubc_ref[...].astype(F32)
    u, bg, cg = ubc[:, :CONV_W], ubc[:, CONV_W:2 * CONV_W], ubc[:, 2 * CONV_W:]
    z = cg * u
    hp = hp_ref[HALO - 1:HALO, :].astype(F32)
    hn = hn_ref[0:1, :].astype(F32)
    z_before = hp[:, 2 * CONV_W:] * hp[:, :CONV_W]
    z_after = hn[:, 2 * CONV_W:] * hn[:, :CONV_W]
    row = lax.broadcasted_iota(jnp.int32, (TM, 1), 0)
    z_prev = jnp.where(row == 0, z_before, pltpu.roll(z, 1, 0))
    z_next = jnp.where(row == TM - 1, z_after, pltpu.roll(z, TM - 1, 0))
    tok = i * TM + row
    seg = jnp.where(tok < N_LAT, SEQ, CTX_LEN)
    pos = jnp.bitwise_and(tok, seg - 1)
    z_prev = jnp.where(pos == 0, 0.0, z_prev)
    z_next = jnp.where(pos == seg - 1, 0.0, z_next)
    conv = (bg * (z_prev * cw_ref[0:1, :] + z * cw_ref[1:2, :] + z_next * cw_ref[2:3, :])).astype(BF16)
    dot = functools.partial(jnp.dot, preferred_element_type=F32)
    for sb in range(TM // SUB):
        rows = slice(sb * SUB, (sb + 1) * SUB)
        y = (dot(conv[rows], wo_ref[:CONV_W, :])
             + dot(yf_ref[rows, :].astype(BF16), wo_ref[CONV_W:CONV_W + FOUR_W, :])
             + dot(att_ref[rows, :], wo_ref[CONV_W + FOUR_W:, :]))
        o_ref[rows, :] = x_ref[rows, :] + mod_ref[2:3, :] * _rms(y, g_ref[...])


def _outproj(n_rows, ubc, yf, att, x, mod_l, g, cw, wo):
    n_tiles = n_rows // TM
    per = TM // HALO
    last_halo = n_rows // HALO - 1
    row = lambda i: (i, 0)
    return pl.pallas_call(
        _outproj_kernel,
        grid=(n_tiles,),
        in_specs=[pl.BlockSpec((TM, 3 * CONV_W), row),
                  pl.BlockSpec((HALO, 3 * CONV_W), lambda i: (jnp.maximum(i * per - 1, 0), 0)),
                  pl.BlockSpec((HALO, 3 * CONV_W), lambda i: (jnp.minimum((i + 1) * per, last_halo), 0)),
                  pl.BlockSpec((TM, FOUR_W), row),
                  pl.BlockSpec((TM, ATT_W), row),
                  pl.BlockSpec((TM, D_MODEL), row),
                  _mod_spec(TM),
                  _full((1, D_MODEL)),
                  _full((3, CONV_W)),
                  _full((MIX_W_ROWS, D_MODEL))],
        out_specs=pl.BlockSpec((TM, D_MODEL), row),
        out_shape=jax.ShapeDtypeStruct((n_rows, D_MODEL), F32),
        compiler_params=_params(1),
        name="out_proj",
    )(ubc, ubc, ubc, yf, att, x, mod_l, g, cw, wo)


FF_CHUNK = 1024


def _ffn_kernel(x_ref, mod_ref, gpre_ref, gpost_ref, wu_ref, wd_ref, o_ref):
    x = x_ref[...]
    h = (_rms(x, gpre_ref[...]) * (1.0 + mod_ref[4:5, :]) + mod_ref[3:4, :]).astype(BF16)
    acc = jnp.zeros((TM_FF, D_MODEL), F32)
    for c in range(D_FF // FF_CHUNK):
        cs = slice(c * FF_CHUNK, (c + 1) * FF_CHUNK)
        a = jnp.maximum(jnp.dot(h, wu_ref[:, cs], preferred_element_type=F32), 0.0)
        acc = acc + jnp.dot((a * a).astype(BF16), wd_ref[cs, :], preferred_element_type=F32)
    o_ref[...] = x + mod_ref[5:6, :] * _rms(acc, gpost_ref[...])


def _ffn(n_rows, x, mod_l, gpre, gpost, wu, wd):
    row = lambda i: (i, 0)
    return pl.pallas_call(
        _ffn_kernel,
        grid=(n_rows // TM_FF,),
        in_specs=[pl.BlockSpec((TM_FF, D_MODEL), row),
                  _mod_spec(TM_FF),
                  _full((1, D_MODEL)),
                  _full((1, D_MODEL)),
                  _full((D_MODEL, D_FF)),
                  _full((D_FF, D_MODEL))],
        out_specs=pl.BlockSpec((TM_FF, D_MODEL), row),
        out_shape=jax.ShapeDtypeStruct((n_rows, D_MODEL), F32),
        compiler_params=_params(1),
        name="ffn",
    )(x, mod_l, gpre, gpost, wu, wd)


def _rope_tables():
    t = jnp.arange(SEQ)
    rowp = (t // GRID_W).astype(F32)
    colp = (t % GRID_W).astype(F32)
    inv = 1.0 / (ROPE_THETA ** (jnp.arange(0, AXIS_DIM, 2, dtype=F32) / AXIS_DIM))
    ang = jnp.stack([rowp[:, None] * inv, colp[:, None] * inv], axis=1)
    ang = jnp.broadcast_to(ang[:, :, None, :], (SEQ, 2, 2, AXIS_DIM // 2)).reshape(SEQ, HEAD_DIM)
    cos, sin = jnp.cos(ang), jnp.sin(ang)
    first_half = (jnp.arange(HEAD_DIM) % AXIS_DIM) < AXIS_DIM // 2
    sin_lo = jnp.where(first_half, -sin, 0.0)
    sin_hi = jnp.where(first_half, 0.0, sin)
    ident = lambda v: jnp.full((TM, HEAD_DIM), v, F32)
    wide = lambda a, v: jnp.tile(jnp.concatenate([a, ident(v)], axis=0), (1, LANES // HEAD_DIM))
    return wide(cos, 1.0), wide(sin_lo, 0.0), wide(sin_hi, 0.0)


def _dft_tables():
    two_pi = 2.0 * np.pi
    n = np.arange(FOUR_GROUP_DIM)
    ang = two_pi * np.outer(n, n) / FOUR_GROUP_DIM
    eye = np.eye(FOUR_GROUPS)
    scale = FOUR_GROUP_DIM ** -0.5
    mch = np.concatenate([np.kron(eye, np.cos(ang)), -np.kron(eye, np.sin(ang))], axis=1) * scale
    r = np.arange(DFT_R)
    ang_r = two_pi * np.outer(r, r) / DFT_R
    cs64 = np.concatenate([np.cos(ang_r), np.sin(ang_r)], axis=0)
    n_groups = DFT_R // F2_GROUP
    m = np.arange(n_groups)[:, None, None, None]
    k2 = np.arange(DFT_R)[None, :, None, None]
    g = np.arange(F2_GROUP)[None, None, :, None]
    c = np.arange(DFT_R)[None, None, None, :]
    k = F2_GROUP * m + g + DFT_R * k2
    ang2 = two_pi * ((k * c) % SEQ) / SEQ
    sel = np.eye(F2_GROUP)[None, None, :, :, None]
    norm2 = SEQ ** -0.5
    tc = (np.cos(ang2)[:, :, :, None, :] * sel * norm2).reshape(n_groups, DFT_R * F2_GROUP, F2_GROUP * DFT_R)
    ts = (np.sin(ang2)[:, :, :, None, :] * sel * norm2).reshape(n_groups, DFT_R * F2_GROUP, F2_GROUP * DFT_R)
    t = np.arange(CTX_LEN)
    ang_c = two_pi * (np.outer(t, t) % CTX_LEN) / CTX_LEN
    normc = CTX_LEN ** -0.5
    c256, s256 = np.cos(ang_c) * normc, np.sin(ang_c) * normc
    as_bf16 = lambda a: jnp.asarray(a, dtype=F32).astype(BF16)
    return tuple(as_bf16(a) for a in (mch, cs64, tc, ts, c256, s256))


def _head_mean_matrix():
    blocks = np.kron(np.eye(N_HEADS), np.full((HEAD_DIM, HEAD_DIM), 1.0 / HEAD_DIM))
    return jnp.asarray(blocks, dtype=F32).astype(BF16)


def kernel(x, c, ctx, c_ctx, w_ada, b_ada, w_in, conv_w, q_gain, k_gain, w_out,
           g_pre_mix, g_post_mix, g_pre_ffn, g_post_ffn, w_ffn_up, w_ffn_down):
    cos, sin_lo, sin_hi = _rope_tables()
    mch, cs64, tc, ts, c256, s256 = _dft_tables()
    ones = _head_mean_matrix()

    c8 = jnp.concatenate([c, c_ctx[None, :], jnp.zeros((MOD_ROWS - BATCH - 1, D_MODEL), F32)], axis=0)
    mod = _ada(c8, w_ada, b_ada).reshape(DEPTH, MOD_ROWS, N_MOD, D_MODEL)

    xs = jnp.concatenate([x.reshape(N_LAT, D_MODEL), ctx.reshape(N_CTX, D_MODEL)], axis=0)
    w_in_b, w_out_b = w_in.astype(BF16), w_out.astype(BF16)
    w_up_b, w_down_b = w_ffn_up.astype(BF16), w_ffn_down.astype(BF16)
    vec = lambda a: a.reshape(1, -1)

    for l in range(DEPTH):
        last = l == DEPTH - 1
        qg = vec(jnp.tile(q_gain[l], N_HEADS))
        kg = vec(jnp.tile(k_gain[l], N_KV))
        ubc, zre, zim, qt, k, vt = _inproj(xs, mod[l], vec(g_pre_mix[l]), w_in_b[l], mch, ones,
                                           qg, kg, cos, sin_lo, sin_hi)
        att = _attn_lat(qt, k, vt)
        yf = _four_lat(zre, zim, cs64, tc, ts)
        if last:
            n_rows = N_LAT
        else:
            n_rows = N_TOK
            att = _attn_ctx(qt, k, vt, att)
            yf = _four_ctx(zre, zim, c256, s256, yf)
        xs = _outproj(n_rows, ubc, yf, att, xs, mod[l], vec(g_post_mix[l]), conv_w[l], w_out_b[l])
        xs = _ffn(n_rows, xs, mod[l], vec(g_pre_ffn[l]), vec(g_post_ffn[l]), w_up_b[l], w_down_b[l])
    return xs.reshape(BATCH, SEQ, D_MODEL)
```

```python
import functools
import math

import jax
import jax.numpy as jnp
import numpy as np
from jax import lax
from jax.experimental import pallas as pl
from jax.experimental.pallas import tpu as pltpu

F32 = jnp.float32
BF16 = jnp.bfloat16

D_MODEL = 1024
BATCH = 4
SEQ = 4096
DEPTH = 4
GRID_W = 64
CTX_LEN = 256
CONV_W = 256
FOUR_GROUPS = 4
FOUR_GROUP_DIM = 64
FOUR_W = FOUR_GROUPS * FOUR_GROUP_DIM
N_HEADS = 8
N_KV = 2
GROUP = N_HEADS // N_KV
HEAD_DIM = 64
ATT_W = N_HEADS * HEAD_DIM
KV_W = N_KV * HEAD_DIM
Q_START = 3 * CONV_W + FOUR_W
K_START = Q_START + ATT_W
V_START = K_START + KV_W
PROJ_W = V_START + KV_W
AXIS_DIM = HEAD_DIM // 2
ROPE_THETA = 10000.0
D_FF = 4 * D_MODEL
N_MOD = 6
EPS = 1e-6

N_LAT = BATCH * SEQ
N_CTX = BATCH * CTX_LEN
N_TOK = N_LAT + N_CTX
MOD_ROWS = 8
CTX_MOD_ROW = BATCH

LANES = 128
BF16_SUBLANES = 16
VMEM_LIMIT_BYTES = 56 * 1024 * 1024

TM = 1024
SUB = 256
TM_FF = 1024
TQ = 256
KT = 256
SCORE_LEAD = 6
LOG2_E = math.log2(math.e)
MIX_W_ROWS = CONV_W + FOUR_W + ATT_W
DFT_R = 64
F2_GROUP = 8
HALO = BF16_SUBLANES
Z_PITCH = DFT_R + 8
Z_ROWS = N_TOK // DFT_R * Z_PITCH

assert SEQ == DFT_R * DFT_R and SEQ % TM == 0 and N_CTX % TM == 0 and SEQ % TM_FF == 0 and N_CTX % TM_FF == 0
assert CTX_LEN == TQ == KT == SUB and SEQ % TQ == 0 and TM % SUB == 0


def _params(n_axes, flags=None):
    return pltpu.CompilerParams(dimension_semantics=("arbitrary",) * n_axes,
                                vmem_limit_bytes=VMEM_LIMIT_BYTES, flags=flags)


def _full(shape):
    return pl.BlockSpec(shape, lambda *_: (0,) * len(shape))


def _layer(l, shape):
    return pl.BlockSpec((None,) + shape, lambda *_: (l,) + (0,) * len(shape))


def _mod_spec(l, tm):
    def index(i):
        return (l, jnp.where(i < N_LAT // tm, i // (SEQ // tm), CTX_MOD_ROW), 0, 0)
    return pl.BlockSpec((None, None, N_MOD, D_MODEL), index)


def _ada_kernel(c_ref, w_ref, b_ref, o_ref):
    c = c_ref[...]
    s = c * jax.nn.sigmoid(c)
    o_ref[...] = jnp.dot(s, w_ref[...], preferred_element_type=F32,
                         precision=lax.Precision.HIGHEST) + b_ref[...]


def _ada(c8, w_ada, b_ada):
    tn = 1024
    n = N_MOD * D_MODEL
    return pl.pallas_call(
        _ada_kernel,
        grid=(DEPTH, n // tn),
        in_specs=[pl.BlockSpec((MOD_ROWS, D_MODEL), lambda l, j: (0, 0)),
                  pl.BlockSpec((None, D_MODEL, tn), lambda l, j: (l, 0, j)),
                  pl.BlockSpec((None, 1, tn), lambda l, j: (l, 0, j))],
        out_specs=pl.BlockSpec((None, MOD_ROWS, tn), lambda l, j: (l, 0, j)),
        out_shape=jax.ShapeDtypeStruct((DEPTH, MOD_ROWS, n), F32),
        compiler_params=_params(2),
        name="ada",
    )(c8, w_ada, b_ada.reshape(DEPTH, 1, n))


def _rms(x, g):
    return x * lax.rsqrt(jnp.mean(x * x, axis=-1, keepdims=True) + EPS) * g


def _head_norm_rope(t, ones, gain, cos, sin_lo, sin_hi):
    w = t.shape[-1]
    ms = jnp.dot((t * t).astype(BF16), ones, preferred_element_type=F32)
    tn = t * lax.rsqrt(ms + EPS) * gain
    up = pltpu.roll(tn, w - AXIS_DIM // 2, 1)
    dn = pltpu.roll(tn, AXIS_DIM // 2, 1)
    return tn * cos + up * sin_lo + dn * sin_hi


def _inproj_kernel(x_ref, mod_ref, g_ref, w_ref, mch_ref, ones_ref, qg_ref, kg_ref,
                   cos_ref, slo_ref, shi_ref,
                   ubc_ref, zre_ref, zim_ref, qt_ref, k_ref, vt_ref):
    rep = ATT_W // LANES
    n_sub = TM // SUB

    def project(sb):
        rows = slice(sb * SUB, (sb + 1) * SUB)
        h = _rms(x_ref[rows, :], g_ref[...]) * (1.0 + mod_ref[1:2, :]) + mod_ref[0:1, :]
        return jnp.dot(h.astype(BF16), w_ref[...], preferred_element_type=F32)

    def finish(sb, p):
        rows = slice(sb * SUB, (sb + 1) * SUB)
        ubc_ref[rows, :] = p[:, :3 * CONV_W].astype(BF16)
        z = jnp.dot(p[:, 3 * CONV_W:Q_START].astype(BF16), mch_ref[...], preferred_element_type=F32)
        for rr in range(SUB // DFT_R):
            src = slice(rr * DFT_R, (rr + 1) * DFT_R)
            dst = pl.ds((sb * (SUB // DFT_R) + rr) * Z_PITCH, DFT_R)
            zre_ref[dst, :] = z[src, :FOUR_W]
            zim_ref[dst, :] = z[src, FOUR_W:]
        cos, slo, shi = cos_ref[rows, :], slo_ref[rows, :], shi_ref[rows, :]
        q = _head_norm_rope(p[:, Q_START:K_START], ones_ref[...], qg_ref[...],
                            jnp.tile(cos, (1, rep)), jnp.tile(slo, (1, rep)), jnp.tile(shi, (1, rep)))
        qt_ref[:, rows] = (q * (HEAD_DIM ** -0.5 * LOG2_E)).T.astype(BF16)
        k = _head_norm_rope(p[:, K_START:V_START], ones_ref[:KV_W, :KV_W], kg_ref[...], cos, slo, shi)
        k_ref[rows, :] = k.astype(BF16)
        vt_ref[sb] = p[:, V_START:].T.astype(BF16)

    for sb in range(n_sub):
        finish(sb, project(sb))


def _inproj(l, x, mod, g, w, mch, ones, qg, kg, cos, slo, shi):
    n_tiles = N_TOK // TM
    n_lat_tiles = N_LAT // TM
    pos_tiles = SEQ // TM

    def pos_map(i):
        return (jnp.where(i < n_lat_tiles, i % pos_tiles, pos_tiles), 0)

    row = lambda i: (i, 0)
    pos_spec = pl.BlockSpec((TM, LANES), pos_map)
    return pl.pallas_call(
        _inproj_kernel,
        grid=(n_tiles,),
        in_specs=[pl.BlockSpec((TM, D_MODEL), row),
                  _mod_spec(l, TM),
                  _layer(l, (1, D_MODEL)),
                  _layer(l, (D_MODEL, PROJ_W)),
                  _full((FOUR_W, 2 * FOUR_W)),
                  _full((ATT_W, ATT_W)),
                  _layer(l, (1, ATT_W)),
                  _layer(l, (1, KV_W)),
                  pos_spec, pos_spec, pos_spec],
        out_specs=[pl.BlockSpec((TM, 3 * CONV_W), row),
                   pl.BlockSpec((TM // DFT_R * Z_PITCH, FOUR_W), row),
                   pl.BlockSpec((TM // DFT_R * Z_PITCH, FOUR_W), row),
                   pl.BlockSpec((ATT_W, TM), lambda i: (0, i)),
                   pl.BlockSpec((TM, KV_W), row),
                   pl.BlockSpec((TM // KT, KV_W, KT), lambda i: (i, 0, 0))],
        out_shape=[jax.ShapeDtypeStruct((N_TOK, 3 * CONV_W), BF16),
                   jax.ShapeDtypeStruct((Z_ROWS, FOUR_W), F32),
                   jax.ShapeDtypeStruct((Z_ROWS, FOUR_W), F32),
                   jax.ShapeDtypeStruct((ATT_W, N_TOK), BF16),
                   jax.ShapeDtypeStruct((N_TOK, KV_W), BF16),
                   jax.ShapeDtypeStruct((N_TOK // KT, KV_W, KT), BF16)],
        compiler_params=_params(1),
        name="in_proj",
    )(x, mod, g, w, mch, ones, qg, kg, cos, slo, shi)


def _attn_body(qt_ref, kc_ref, vtc_ref, kl_ref, vtl_ref, o_ref, ot_ref, *, n_lat_tiles):
    sub = 8
    ones = jnp.ones((BF16_SUBLANES, KT), BF16)

    n_tiles = 1 + n_lat_tiles
    work = [(g, j) for g in range(N_HEADS) for j in range(n_tiles)]

    def scores(g, j):
        qt = qt_ref[g * HEAD_DIM:(g + 1) * HEAD_DIM, :]
        zero = jnp.zeros_like(qt)
        w = jnp.concatenate([qt, zero] if g // GROUP == 0 else [zero, qt], axis=0)
        k_tile = kc_ref[...] if j == 0 else kl_ref[(j - 1) * KT:j * KT, :]
        return jnp.dot(k_tile, w, preferred_element_type=F32)

    def accumulate(g, j, s, m, acc):
        hv = g // GROUP
        v_rows = slice(hv * HEAD_DIM, (hv + 1) * HEAD_DIM)
        vt_tile = vtc_ref[0, v_rows, :] if j == 0 else vtl_ref[j - 1, v_rows, :]
        tile_max = jnp.max(jnp.max(s.reshape(KT // sub, sub, TQ), axis=0), axis=0, keepdims=True)
        m_new = tile_max if j == 0 else jnp.maximum(m, tile_max)
        p = jnp.exp2(s - m_new).astype(BF16)
        lhs = jnp.concatenate([vt_tile, ones], axis=0)
        pv = jnp.dot(lhs, p, preferred_element_type=F32)
        if j > 0:
            pv = pv + acc * jnp.exp2(m - m_new)
        return m_new, pv

    pending = {}
    m = acc = None
    for step in range(len(work) + SCORE_LEAD):
        if step < len(work):
            pending[step] = scores(*work[step])
        if step >= SCORE_LEAD:
            g, j = work[step - SCORE_LEAD]
            m, acc = accumulate(g, j, pending.pop(step - SCORE_LEAD), m, acc)
            if j == n_tiles - 1:
                ot_ref[g * HEAD_DIM:(g + 1) * HEAD_DIM, :] = acc[:HEAD_DIM] / acc[HEAD_DIM:HEAD_DIM + 1]
    o_ref[...] = ot_ref[...].T.astype(o_ref.dtype)


def _attn_lat_kernel(qt_ref, kc_ref, vtc_ref, kl_ref, vtl_ref, wu_ref, wd_ref,
                     o_ref, wu_out_ref, wd_out_ref, ot_ref):
    _attn_body(qt_ref, kc_ref, vtc_ref, kl_ref, vtl_ref, o_ref, ot_ref, n_lat_tiles=SEQ // KT)
    wu_out_ref[...] = wu_ref[...].astype(BF16)
    wd_out_ref[...] = wd_ref[...].astype(BF16)


def _attn_ctx_kernel(qt_ref, kc_ref, vtc_ref, prev_ref, o_ref, ot_ref):
    del prev_ref
    _attn_body(qt_ref, kc_ref, vtc_ref, None, None, o_ref, ot_ref, n_lat_tiles=0)


def _attn_scratch():
    return [pltpu.VMEM((ATT_W, TQ), F32)]


def _attn_lat(l, qt, k, vt, w_up, w_down):
    nq = SEQ // TQ
    n_steps = BATCH * nq
    ctx0 = N_LAT // CTX_LEN
    up_rows, down_rows = D_MODEL // n_steps, D_FF // n_steps
    up_spec = lambda lead: pl.BlockSpec(lead + (up_rows, D_FF), lambda b, j: (l,) * len(lead) + (b * nq + j, 0))
    down_spec = lambda lead: pl.BlockSpec(lead + (down_rows, D_MODEL),
                                          lambda b, j: (l,) * len(lead) + (b * nq + j, 0))
    return pl.pallas_call(
        _attn_lat_kernel,
        grid=(BATCH, nq),
        in_specs=[pl.BlockSpec((ATT_W, TQ), lambda b, j: (0, b * nq + j)),
                  pl.BlockSpec((CTX_LEN, KV_W), lambda b, j: (ctx0 + b, 0)),
                  pl.BlockSpec((CTX_LEN // KT, KV_W, KT), lambda b, j: (ctx0 + b, 0, 0)),
                  pl.BlockSpec((SEQ, KV_W), lambda b, j: (b, 0)),
                  pl.BlockSpec((SEQ // KT, KV_W, KT), lambda b, j: (b, 0, 0)),
                  up_spec((None,)), down_spec((None,))],
        out_specs=[pl.BlockSpec((TQ, ATT_W), lambda b, j: (b * nq + j, 0)), up_spec(()), down_spec(())],
        out_shape=[jax.ShapeDtypeStruct((N_TOK, ATT_W), BF16),
                   jax.ShapeDtypeStruct((D_MODEL, D_FF), BF16),
                   jax.ShapeDtypeStruct((D_FF, D_MODEL), BF16)],
        scratch_shapes=_attn_scratch(),
        compiler_params=_params(2),
        name="attn_latent",
    )(qt, k, vt, k, vt, w_up, w_down)


def _attn_ctx(qt, k, vt, att):
    ctx0 = N_LAT // CTX_LEN
    return pl.pallas_call(
        _attn_ctx_kernel,
        grid=(BATCH,),
        in_specs=[pl.BlockSpec((ATT_W, TQ), lambda b: (0, ctx0 + b)),
                  pl.BlockSpec((CTX_LEN, KV_W), lambda b: (ctx0 + b, 0)),
                  pl.BlockSpec((CTX_LEN // KT, KV_W, KT), lambda b: (ctx0 + b, 0, 0)),
                  pl.BlockSpec(memory_space=pl.ANY)],
        out_specs=pl.BlockSpec((TQ, ATT_W), lambda b: (ctx0 + b, 0)),
        out_shape=jax.ShapeDtypeStruct((N_TOK, ATT_W), BF16),
        input_output_aliases={3: 0},
        scratch_shapes=_attn_scratch(),
        compiler_params=_params(1),
        name="attn_context",
    )(qt, k, vt, att)


def _four_lat_kernel(zre0_ref, zre1_ref, zim0_ref, zim1_ref, cs_ref, tc_ref, ts_ref, o_ref,
                     are0_ref, are1_ref, aim0_ref, aim1_ref):
    cs = cs_ref[...]
    both = lambda lo_ref, hi_ref, idx: jnp.concatenate([lo_ref[idx, :], hi_ref[idx, :]], axis=1)
    for c in range(DFT_R):
        strided = pl.ds(c, DFT_R, stride=Z_PITCH)
        pr = jnp.dot(cs, both(zre0_ref, zre1_ref, strided).astype(BF16), preferred_element_type=F32)
        pi = jnp.dot(cs, both(zim0_ref, zim1_ref, strided).astype(BF16), preferred_element_type=F32)
        are = pr[:DFT_R] + pi[DFT_R:]
        aim = pi[:DFT_R] - pr[DFT_R:]
        are0_ref[strided, :] = are[:, :LANES]
        are1_ref[strided, :] = are[:, LANES:]
        aim0_ref[strided, :] = aim[:, :LANES]
        aim1_ref[strided, :] = aim[:, LANES:]

    def group(lo_ref, hi_ref, m):
        parts = [both(lo_ref, hi_ref, pl.ds((m * F2_GROUP + g) * Z_PITCH, DFT_R)) for g in range(F2_GROUP)]
        return jnp.concatenate(parts, axis=0).astype(BF16)

    for m in range(DFT_R // F2_GROUP):
        y = (jnp.dot(tc_ref[m], group(are0_ref, are1_ref, m), preferred_element_type=F32)
             + jnp.dot(ts_ref[m], group(aim0_ref, aim1_ref, m), preferred_element_type=F32))
        o_ref[:, m, :, :] = y.reshape(DFT_R, F2_GROUP, FOUR_W)


def _four_lat(zre, zim, cs64, tc, ts):
    n_groups = DFT_R // F2_GROUP
    rows = F2_GROUP * DFT_R
    seq_rows = DFT_R * Z_PITCH
    lo = pl.BlockSpec((seq_rows, LANES), lambda b: (b, 0))
    hi = pl.BlockSpec((seq_rows, LANES), lambda b: (b, 1))
    t_spec = pl.BlockSpec((n_groups, rows, rows), lambda b: (0, 0, 0), pipeline_mode=pl.Buffered(1))
    yf = pl.pallas_call(
        _four_lat_kernel,
        grid=(BATCH,),
        in_specs=[lo, hi, lo, hi, _full((2 * DFT_R, DFT_R)), t_spec, t_spec],
        out_specs=pl.BlockSpec((DFT_R, n_groups, F2_GROUP, FOUR_W), lambda b: (b, 0, 0, 0)),
        out_shape=jax.ShapeDtypeStruct((N_TOK // DFT_R, n_groups, F2_GROUP, FOUR_W), F32),
        scratch_shapes=[pltpu.VMEM((seq_rows, LANES), F32)] * 4,
        compiler_params=_params(1),
        name="fourier_latent",
    )(zre, zre, zim, zim, cs64, tc, ts)
    return yf.reshape(N_TOK, FOUR_W)


def _four_ctx_kernel(zre_ref, zim_ref, c_ref, s_ref, prev_ref, o_ref):
    del prev_ref
    rows = lambda ref: jnp.concatenate(
        [ref[pl.ds(r * Z_PITCH, DFT_R), :] for r in range(CTX_LEN // DFT_R)], axis=0).astype(BF16)
    o_ref[...] = (jnp.dot(c_ref[...], rows(zre_ref), preferred_element_type=F32)
                  + jnp.dot(s_ref[...], rows(zim_ref), preferred_element_type=F32))


def _four_ctx(zre, zim, c256, s256, yf):
    ctx0 = N_LAT // CTX_LEN
    z_spec = pl.BlockSpec((CTX_LEN // DFT_R * Z_PITCH, FOUR_W), lambda b: (ctx0 + b, 0))
    blk = pl.BlockSpec((CTX_LEN, FOUR_W), lambda b: (ctx0 + b, 0))
    return pl.pallas_call(
        _four_ctx_kernel,
        grid=(BATCH,),
        in_specs=[z_spec, z_spec, _full((CTX_LEN, CTX_LEN)), _full((CTX_LEN, CTX_LEN)),
                  pl.BlockSpec(memory_space=pl.ANY)],
        out_specs=blk,
        out_shape=jax.ShapeDtypeStruct((N_TOK, FOUR_W), F32),
        input_output_aliases={4: 0},
        compiler_params=_params(1),
        name="fourier_context",
    )(zre, zim, c256, s256, yf)


def _outproj_kernel(ubc_ref, hp_ref, hn_ref, yf_ref, att_ref, x_ref, mod_ref, g_ref, cw_ref, wo_ref, o_ref):
    i = pl.program_id(0)
    ubc = ubc_ref[...].astype(F32)
    u, bg, cg = ubc[:, :CONV_W], ubc[:, CONV_W:2 * CONV_W], ubc[:, 2 * CONV_W:]
    z = cg * u
    hp = hp_ref[HALO - 1:HALO, :].astype(F32)
    hn = hn_ref[0:1, :].astype(F32)
    z_before = hp[:, 2 * CONV_W:] * hp[:, :CONV_W]
    z_after = hn[:, 2 * CONV_W:] * hn[:, :CONV_W]
    row = lax.broadcasted_iota(jnp.int32, (TM, 1), 0)
    z_prev = jnp.where(row == 0, z_before, pltpu.roll(z, 1, 0))
    z_next = jnp.where(row == TM - 1, z_after, pltpu.roll(z, TM - 1, 0))
    tok = i * TM + row
    seg = jnp.where(tok < N_LAT, SEQ, CTX_LEN)
    pos = jnp.bitwise_and(tok, seg - 1)
    z_prev = jnp.where(pos == 0, 0.0, z_prev)
    z_next = jnp.where(pos == seg - 1, 0.0, z_next)
    conv = (bg * (z_prev * cw_ref[0:1, :] + z * cw_ref[1:2, :] + z_next * cw_ref[2:3, :])).astype(BF16)
    dot = functools.partial(jnp.dot, preferred_element_type=F32)
    for sb in range(TM // SUB):
        rows = slice(sb * SUB, (sb + 1) * SUB)
        y = (dot(att_ref[rows, :], wo_ref[CONV_W + FOUR_W:, :])
             + dot(yf_ref[rows, :].astype(BF16), wo_ref[CONV_W:CONV_W + FOUR_W, :])
             + dot(conv[rows], wo_ref[:CONV_W, :]))
        o_ref[rows, :] = x_ref[rows, :] + mod_ref[2:3, :] * _rms(y, g_ref[...])


def _outproj(l, n_rows, ubc, yf, att, x, mod, g, cw, wo):
    n_tiles = n_rows // TM
    per = TM // HALO
    last_halo = n_rows // HALO - 1
    row = lambda i: (i, 0)
    return pl.pallas_call(
        _outproj_kernel,
        grid=(n_tiles,),
        in_specs=[pl.BlockSpec((TM, 3 * CONV_W), row),
                  pl.BlockSpec((HALO, 3 * CONV_W), lambda i: (jnp.maximum(i * per - 1, 0), 0)),
                  pl.BlockSpec((HALO, 3 * CONV_W), lambda i: (jnp.minimum((i + 1) * per, last_halo), 0)),
                  pl.BlockSpec((TM, FOUR_W), row),
                  pl.BlockSpec((TM, ATT_W), row),
                  pl.BlockSpec((TM, D_MODEL), row),
                  _mod_spec(l, TM),
                  _layer(l, (1, D_MODEL)),
                  _layer(l, (3, CONV_W)),
                  _layer(l, (MIX_W_ROWS, D_MODEL))],
        out_specs=pl.BlockSpec((TM, D_MODEL), row),
        out_shape=jax.ShapeDtypeStruct((n_rows, D_MODEL), F32),
        compiler_params=_params(1),
        name="out_proj",
    )(ubc, ubc, ubc, yf, att, x, mod, g, cw, wo)


FF_CHUNK = 1024
FF_LEAD = 1


def _ffn_kernel(x_ref, mod_ref, gpre_ref, gpost_ref, wu_ref, wd_ref, o_ref):
    n_chunks = D_FF // FF_CHUNK
    work = [(sb, c) for sb in range(TM_FF // SUB) for c in range(n_chunks)]

    normed = {}

    def up(sb, c):
        if sb not in normed:
            x = x_ref[sb * SUB:(sb + 1) * SUB, :]
            normed[sb] = (_rms(x, gpre_ref[...]) * (1.0 + mod_ref[4:5, :]) + mod_ref[3:4, :]).astype(BF16)
        return jnp.dot(normed[sb], wu_ref[:, c * FF_CHUNK:(c + 1) * FF_CHUNK], preferred_element_type=F32)

    def down(c, a):
        a = jnp.maximum(a, 0.0)
        return jnp.dot((a * a).astype(BF16), wd_ref[c * FF_CHUNK:(c + 1) * FF_CHUNK, :],
                       preferred_element_type=F32)

    pending = {}
    acc = None
    for step in range(len(work) + FF_LEAD):
        if step < len(work):
            pending[step] = up(*work[step])
        if step >= FF_LEAD:
            sb, c = work[step - FF_LEAD]
            d = down(c, pending.pop(step - FF_LEAD))
            acc = d if c == 0 else acc + d
            if c == n_chunks - 1:
                rows = slice(sb * SUB, (sb + 1) * SUB)
                o_ref[rows, :] = x_ref[rows, :] + mod_ref[5:6, :] * _rms(acc, gpost_ref[...])


def _ffn(l, n_rows, x, mod, gpre, gpost, wu, wd):
    row = lambda i: (i, 0)
    return pl.pallas_call(
        _ffn_kernel,
        grid=(n_rows // TM_FF,),
        in_specs=[pl.BlockSpec((TM_FF, D_MODEL), row),
                  _mod_spec(l, TM_FF),
                  _layer(l, (1, D_MODEL)),
                  _layer(l, (1, D_MODEL)),
                  pl.BlockSpec((D_MODEL, D_FF), lambda i: (0, 0), pipeline_mode=pl.Buffered(1)),
                  pl.BlockSpec((D_FF, D_MODEL), lambda i: (0, 0), pipeline_mode=pl.Buffered(1))],
        out_specs=pl.BlockSpec((TM_FF, D_MODEL), row),
        out_shape=jax.ShapeDtypeStruct((n_rows, D_MODEL), F32),
        compiler_params=_params(1),
        name="ffn",
    )(x, mod, gpre, gpost, wu, wd)


def _rope_tables():
    t = jnp.arange(SEQ)
    rowp = (t // GRID_W).astype(F32)
    colp = (t % GRID_W).astype(F32)
    inv = 1.0 / (ROPE_THETA ** (jnp.arange(0, AXIS_DIM, 2, dtype=F32) / AXIS_DIM))
    ang = jnp.stack([rowp[:, None] * inv, colp[:, None] * inv], axis=1)
    ang = jnp.broadcast_to(ang[:, :, None, :], (SEQ, 2, 2, AXIS_DIM // 2)).reshape(SEQ, HEAD_DIM)
    cos, sin = jnp.cos(ang), jnp.sin(ang)
    first_half = (jnp.arange(HEAD_DIM) % AXIS_DIM) < AXIS_DIM // 2
    sin_lo = jnp.where(first_half, -sin, 0.0)
    sin_hi = jnp.where(first_half, 0.0, sin)
    ident = lambda v: jnp.full((TM, HEAD_DIM), v, F32)
    wide = lambda a, v: jnp.tile(jnp.concatenate([a, ident(v)], axis=0), (1, LANES // HEAD_DIM))
    return wide(cos, 1.0), wide(sin_lo, 0.0), wide(sin_hi, 0.0)


def _dft_tables():
    two_pi = 2.0 * np.pi
    n = np.arange(FOUR_GROUP_DIM)
    ang = two_pi * np.outer(n, n) / FOUR_GROUP_DIM
    eye = np.eye(FOUR_GROUPS)
    scale = FOUR_GROUP_DIM ** -0.5
    mch = np.concatenate([np.kron(eye, np.cos(ang)), -np.kron(eye, np.sin(ang))], axis=1) * scale
    r = np.arange(DFT_R)
    ang_r = two_pi * np.outer(r, r) / DFT_R
    cs64 = np.concatenate([np.cos(ang_r), np.sin(ang_r)], axis=0)
    n_groups = DFT_R // F2_GROUP
    m = np.arange(n_groups)[:, None, None, None]
    k2 = np.arange(DFT_R)[None, :, None, None]
    g = np.arange(F2_GROUP)[None, None, :, None]
    c = np.arange(DFT_R)[None, None, None, :]
    k = F2_GROUP * m + g + DFT_R * k2
    ang2 = two_pi * ((k * c) % SEQ) / SEQ
    sel = np.eye(F2_GROUP)[None, None, :, :, None]
    norm2 = SEQ ** -0.5
    tc = (np.cos(ang2)[:, :, :, None, :] * sel * norm2).reshape(n_groups, DFT_R * F2_GROUP, F2_GROUP * DFT_R)
    ts = (np.sin(ang2)[:, :, :, None, :] * sel * norm2).reshape(n_groups, DFT_R * F2_GROUP, F2_GROUP * DFT_R)
    t = np.arange(CTX_LEN)
    ang_c = two_pi * (np.outer(t, t) % CTX_LEN) / CTX_LEN
    normc = CTX_LEN ** -0.5
    c256, s256 = np.cos(ang_c) * normc, np.sin(ang_c) * normc
    as_bf16 = lambda a: jnp.asarray(a, dtype=F32).astype(BF16)
    return tuple(as_bf16(a) for a in (mch, cs64, tc, ts, c256, s256))


def _head_mean_matrix():
    blocks = np.kron(np.eye(N_HEADS), np.full((HEAD_DIM, HEAD_DIM), 1.0 / HEAD_DIM))
    return jnp.asarray(blocks, dtype=F32).astype(BF16)


def kernel(x, c, ctx, c_ctx, w_ada, b_ada, w_in, conv_w, q_gain, k_gain, w_out,
           g_pre_mix, g_post_mix, g_pre_ffn, g_post_ffn, w_ffn_up, w_ffn_down):
    cos, sin_lo, sin_hi = _rope_tables()
    mch, cs64, tc, ts, c256, s256 = _dft_tables()
    ones = _head_mean_matrix()

    c8 = jnp.concatenate([c, c_ctx[None, :], jnp.zeros((MOD_ROWS - BATCH - 1, D_MODEL), F32)], axis=0)
    mod = _ada(c8, w_ada, b_ada).reshape(DEPTH, MOD_ROWS, N_MOD, D_MODEL)

    xs = jnp.concatenate([x.reshape(N_LAT, D_MODEL), ctx.reshape(N_CTX, D_MODEL)], axis=0)
    w_in_b, w_out_b = w_in.astype(BF16), w_out.astype(BF16)
    rows = lambda a: a.reshape(DEPTH, 1, -1)
    qg, kg = rows(jnp.tile(q_gain, (1, N_HEADS))), rows(jnp.tile(k_gain, (1, N_KV)))
    g_pre_mix, g_post_mix, g_pre_ffn, g_post_ffn = map(rows, (g_pre_mix, g_post_mix, g_pre_ffn, g_post_ffn))

    for l in range(DEPTH):
        last = l == DEPTH - 1
        ubc, zre, zim, qt, k, vt = _inproj(l, xs, mod, g_pre_mix, w_in_b, mch, ones,
                                           qg, kg, cos, sin_lo, sin_hi)
        att, w_up_b, w_down_b = _attn_lat(l, qt, k, vt, w_ffn_up, w_ffn_down)
        yf = _four_lat(zre, zim, cs64, tc, ts)
        if last:
            n_rows = N_LAT
        else:
            n_rows = N_TOK
            att = _attn_ctx(qt, k, vt, att)
            yf = _four_ctx(zre, zim, c256, s256, yf)
        xs = _outproj(l, n_rows, ubc, yf, att, xs, mod, g_post_mix, conv_w, w_out_b)
        xs = _ffn(l, n_rows, xs, mod, g_pre_ffn, g_post_ffn, w_up_b, w_down_b)
    return xs.reshape(BATCH, SEQ, D_MODEL)
```

```python
import functools
import math

import jax
import jax.numpy as jnp
import numpy as np
from jax import lax
from jax.experimental import pallas as pl
from jax.experimental.pallas import tpu as pltpu

F32 = jnp.float32
BF16 = jnp.bfloat16

D_MODEL = 1024
BATCH = 4
SEQ = 4096
DEPTH = 4
GRID_W = 64
CTX_LEN = 256
CONV_W = 256
FOUR_GROUPS = 4
FOUR_GROUP_DIM = 64
FOUR_W = FOUR_GROUPS * FOUR_GROUP_DIM
N_HEADS = 8
N_KV = 2
GROUP = N_HEADS // N_KV
HEAD_DIM = 64
ATT_W = N_HEADS * HEAD_DIM
KV_W = N_KV * HEAD_DIM
Q_START = 3 * CONV_W + FOUR_W
K_START = Q_START + ATT_W
V_START = K_START + KV_W
PROJ_W = V_START + KV_W
AXIS_DIM = HEAD_DIM // 2
ROPE_THETA = 10000.0
D_FF = 4 * D_MODEL
N_MOD = 6
EPS = 1e-6

N_LAT = BATCH * SEQ
N_CTX = BATCH * CTX_LEN
N_TOK = N_LAT + N_CTX
MOD_ROWS = 8
CTX_MOD_ROW = BATCH

LANES = 128
BF16_SUBLANES = 16
VMEM_LIMIT_BYTES = 56 * 1024 * 1024

TM = 1024
SUB = 256
TM_FF = 1024
TQ = 256
KT = 256
Q_PER_STEP = 2
SCORE_LEAD = 6
LOG2_E = math.log2(math.e)
MIX_W_ROWS = CONV_W + FOUR_W + ATT_W
DFT_R = 64
F2_GROUP = 8
HALO = BF16_SUBLANES
Z_PITCH = DFT_R + 8
Z_ROWS = N_TOK // DFT_R * Z_PITCH

assert SEQ == DFT_R * DFT_R and SEQ % TM == 0 and N_CTX % TM == 0 and SEQ % TM_FF == 0 and N_CTX % TM_FF == 0
assert CTX_LEN == TQ == KT == SUB and SEQ % TQ == 0 and TM % SUB == 0


def _params(n_axes, flags=None):
    return pltpu.CompilerParams(dimension_semantics=("arbitrary",) * n_axes,
                                vmem_limit_bytes=VMEM_LIMIT_BYTES, flags=flags)


def _full(shape):
    return pl.BlockSpec(shape, lambda *_: (0,) * len(shape))


def _layer(l, shape):
    return pl.BlockSpec((None,) + shape, lambda *_: (l,) + (0,) * len(shape))


def _mod_spec(l, tm):
    def index(i):
        return (l, jnp.where(i < N_LAT // tm, i // (SEQ // tm), CTX_MOD_ROW), 0, 0)
    return pl.BlockSpec((None, None, N_MOD, D_MODEL), index)


def _ada_kernel(c_ref, w_ref, b_ref, o_ref):
    c = c_ref[...]
    s = c * jax.nn.sigmoid(c)
    split = lambda a: (a.astype(BF16), (a - a.astype(BF16).astype(F32)).astype(BF16))
    s_hi, s_lo = split(s)
    w_hi, w_lo = split(w_ref[...])
    head = jnp.dot(jnp.concatenate([s_hi, s_lo], axis=0), w_hi, preferred_element_type=F32)
    tail = jnp.dot(s_hi, w_lo, preferred_element_type=F32)
    o_ref[...] = head[:MOD_ROWS] + head[MOD_ROWS:] + tail + b_ref[...]


def _ada(c8, w_ada, b_ada):
    tn = 1024
    n = N_MOD * D_MODEL
    return pl.pallas_call(
        _ada_kernel,
        grid=(DEPTH, n // tn),
        in_specs=[pl.BlockSpec((MOD_ROWS, D_MODEL), lambda l, j: (0, 0)),
                  pl.BlockSpec((None, D_MODEL, tn), lambda l, j: (l, 0, j)),
                  pl.BlockSpec((None, 1, tn), lambda l, j: (l, 0, j))],
        out_specs=pl.BlockSpec((None, MOD_ROWS, tn), lambda l, j: (l, 0, j)),
        out_shape=jax.ShapeDtypeStruct((DEPTH, MOD_ROWS, n), F32),
        compiler_params=_params(2),
        name="ada",
    )(c8, w_ada, b_ada.reshape(DEPTH, 1, n))


def _rms(x, g):
    return x * lax.rsqrt(jnp.mean(x * x, axis=-1, keepdims=True) + EPS) * g


def _head_norm_rope(t, ones, gain, cos, sin_lo, sin_hi):
    w = t.shape[-1]
    ms = jnp.dot((t * t).astype(BF16), ones, preferred_element_type=F32)
    tn = t * lax.rsqrt(ms + EPS) * gain
    up = pltpu.roll(tn, w - AXIS_DIM // 2, 1)
    dn = pltpu.roll(tn, AXIS_DIM // 2, 1)
    return tn * cos + up * sin_lo + dn * sin_hi


def _token_rows(x_refs):
    if len(x_refs) == 1:
        return lambda rows: x_refs[0][rows, :]
    lat_ref, ctx_ref = x_refs
    is_lat = pl.program_id(0) < N_LAT // TM
    return lambda rows: jnp.where(is_lat, lat_ref[rows, :], ctx_ref[rows, :])


def _token_specs(split):
    row = lambda i: (i, 0)
    if not split:
        return [pl.BlockSpec((TM, D_MODEL), row)]
    n_lat_tiles = N_LAT // TM
    return [pl.BlockSpec((TM, D_MODEL), lambda i: (jnp.minimum(i, n_lat_tiles - 1), 0)),
            pl.BlockSpec((TM, D_MODEL), lambda i: (jnp.maximum(i - n_lat_tiles, 0), 0))]


def _inproj_kernel(*refs, n_x):
    (mod_ref, g_ref, w_ref, mch_ref, ones_ref, qg_ref, kg_ref, cos_ref, slo_ref, shi_ref,
     ubc_ref, zre_ref, zim_ref, qt_ref, k_ref, vt_ref) = refs[n_x:]
    read_x = _token_rows(refs[:n_x])
    rep = ATT_W // LANES
    n_sub = TM // SUB

    def project(sb):
        rows = slice(sb * SUB, (sb + 1) * SUB)
        h = _rms(read_x(rows), g_ref[...]) * (1.0 + mod_ref[1:2, :]) + mod_ref[0:1, :]
        return jnp.dot(h.astype(BF16), w_ref[...], preferred_element_type=F32)

    def finish(sb, p):
        rows = slice(sb * SUB, (sb + 1) * SUB)
        ubc_ref[rows, :] = p[:, :3 * CONV_W].astype(BF16)
        z = jnp.dot(p[:, 3 * CONV_W:Q_START].astype(BF16), mch_ref[...], preferred_element_type=F32)
        for rr in range(SUB // DFT_R):
            src = slice(rr * DFT_R, (rr + 1) * DFT_R)
            dst = pl.ds((sb * (SUB // DFT_R) + rr) * Z_PITCH, DFT_R)
            zre_ref[dst, :] = z[src, :FOUR_W]
            zim_ref[dst, :] = z[src, FOUR_W:]
        cos, slo, shi = cos_ref[rows, :], slo_ref[rows, :], shi_ref[rows, :]
        q = _head_norm_rope(p[:, Q_START:K_START], ones_ref[...], qg_ref[...],
                            jnp.tile(cos, (1, rep)), jnp.tile(slo, (1, rep)), jnp.tile(shi, (1, rep)))
        qt_ref[:, rows] = (q * (HEAD_DIM ** -0.5 * LOG2_E)).T.astype(BF16)
        k = _head_norm_rope(p[:, K_START:V_START], ones_ref[:KV_W, :KV_W], kg_ref[...], cos, slo, shi)
        k_ref[rows, :] = k.astype(BF16)
        vt_ref[sb] = p[:, V_START:].T.astype(BF16)

    for sb in range(n_sub):
        finish(sb, project(sb))


def _inproj(l, xs, mod, g, w, mch, ones, qg, kg, cos, slo, shi):
    n_tiles = N_TOK // TM
    n_lat_tiles = N_LAT // TM
    pos_tiles = SEQ // TM

    def pos_map(i):
        return (jnp.where(i < n_lat_tiles, i % pos_tiles, pos_tiles), 0)

    row = lambda i: (i, 0)
    pos_spec = pl.BlockSpec((TM, LANES), pos_map)
    return pl.pallas_call(
        functools.partial(_inproj_kernel, n_x=len(xs)),
        grid=(n_tiles,),
        in_specs=_token_specs(len(xs) == 2) + [
                  _mod_spec(l, TM),
                  _layer(l, (1, D_MODEL)),
                  _layer(l, (D_MODEL, PROJ_W)),
                  _full((FOUR_W, 2 * FOUR_W)),
                  _full((ATT_W, ATT_W)),
                  _layer(l, (1, ATT_W)),
                  _layer(l, (1, KV_W)),
                  pos_spec, pos_spec, pos_spec],
        out_specs=[pl.BlockSpec((TM, 3 * CONV_W), row),
                   pl.BlockSpec((TM // DFT_R * Z_PITCH, FOUR_W), row),
                   pl.BlockSpec((TM // DFT_R * Z_PITCH, FOUR_W), row),
                   pl.BlockSpec((ATT_W, TM), lambda i: (0, i)),
                   pl.BlockSpec((TM, KV_W), row),
                   pl.BlockSpec((TM // KT, KV_W, KT), lambda i: (i, 0, 0))],
        out_shape=[jax.ShapeDtypeStruct((N_TOK, 3 * CONV_W), BF16),
                   jax.ShapeDtypeStruct((Z_ROWS, FOUR_W), F32),
                   jax.ShapeDtypeStruct((Z_ROWS, FOUR_W), F32),
                   jax.ShapeDtypeStruct((ATT_W, N_TOK), BF16),
                   jax.ShapeDtypeStruct((N_TOK, KV_W), BF16),
                   jax.ShapeDtypeStruct((N_TOK // KT, KV_W, KT), BF16)],
        compiler_params=_params(1),
        name="in_proj",
    )(*xs, mod, g, w, mch, ones, qg, kg, cos, slo, shi)


def _attn_body(qt_ref, kc_ref, vtc_ref, kl_ref, vtl_ref, o_ref, ot_ref, *, n_lat_tiles):
    sub = 8
    ones = jnp.ones((BF16_SUBLANES, KT), BF16)

    n_tiles = 1 + n_lat_tiles
    n_q = qt_ref.shape[1] // TQ
    work = [(qi, g, j) for qi in range(n_q) for g in range(N_HEADS) for j in range(n_tiles)]

    def scores(qi, g, j):
        qt = qt_ref[g * HEAD_DIM:(g + 1) * HEAD_DIM, qi * TQ:(qi + 1) * TQ]
        zero = jnp.zeros_like(qt)
        w = jnp.concatenate([qt, zero] if g // GROUP == 0 else [zero, qt], axis=0)
        k_tile = kc_ref[...] if j == 0 else kl_ref[(j - 1) * KT:j * KT, :]
        return jnp.dot(k_tile, w, preferred_element_type=F32)

    def accumulate(g, j, s, m, acc):
        hv = g // GROUP
        v_rows = slice(hv * HEAD_DIM, (hv + 1) * HEAD_DIM)
        vt_tile = vtc_ref[0, v_rows, :] if j == 0 else vtl_ref[j - 1, v_rows, :]
        tile_max = jnp.max(jnp.max(s.reshape(KT // sub, sub, TQ), axis=0), axis=0, keepdims=True)
        m_new = tile_max if j == 0 else jnp.maximum(m, tile_max)
        p = jnp.exp2(s - m_new).astype(BF16)
        lhs = jnp.concatenate([vt_tile, ones], axis=0)
        pv = jnp.dot(lhs, p, preferred_element_type=F32)
        if j > 0:
            pv = pv + acc * jnp.exp2(m - m_new)
        return m_new, pv

    pending = {}
    m = acc = None
    for step in range(len(work) + SCORE_LEAD):
        if step < len(work):
            pending[step] = scores(*work[step])
        if step >= SCORE_LEAD:
            qi, g, j = work[step - SCORE_LEAD]
            m, acc = accumulate(g, j, pending.pop(step - SCORE_LEAD), m, acc)
            if j == n_tiles - 1:
                ot_ref[g * HEAD_DIM:(g + 1) * HEAD_DIM, :] = acc[:HEAD_DIM] / acc[HEAD_DIM:HEAD_DIM + 1]
                if g == N_HEADS - 1:
                    o_ref[qi * TQ:(qi + 1) * TQ, :] = ot_ref[...].T.astype(o_ref.dtype)


def _attn_lat_kernel(qt_ref, kc_ref, vtc_ref, kl_ref, vtl_ref, wu_ref, wd_ref,
                     o_ref, wu_out_ref, wd_out_ref, ot_ref):
    _attn_body(qt_ref, kc_ref, vtc_ref, kl_ref, vtl_ref, o_ref, ot_ref, n_lat_tiles=SEQ // KT)
    wu_out_ref[...] = wu_ref[...].astype(BF16)
    wd_out_ref[...] = wd_ref[...].astype(BF16)


def _attn_ctx_kernel(qt_ref, kc_ref, vtc_ref, prev_ref, o_ref, ot_ref):
    del prev_ref
    _attn_body(qt_ref, kc_ref, vtc_ref, None, None, o_ref, ot_ref, n_lat_tiles=0)


def _attn_scratch():
    return [pltpu.VMEM((ATT_W, TQ), F32)]


def _attn_lat(l, qt, k, vt, w_up, w_down):
    tq = TQ * Q_PER_STEP
    nq = SEQ // tq
    n_steps = BATCH * nq
    ctx0 = N_LAT // CTX_LEN
    up_rows, down_rows = D_MODEL // n_steps, D_FF // n_steps
    up_spec = lambda lead: pl.BlockSpec(lead + (up_rows, D_FF), lambda b, j: (l,) * len(lead) + (b * nq + j, 0))
    down_spec = lambda lead: pl.BlockSpec(lead + (down_rows, D_MODEL),
                                          lambda b, j: (l,) * len(lead) + (b * nq + j, 0))
    return pl.pallas_call(
        _attn_lat_kernel,
        grid=(BATCH, nq),
        in_specs=[pl.BlockSpec((ATT_W, tq), lambda b, j: (0, b * nq + j)),
                  pl.BlockSpec((CTX_LEN, KV_W), lambda b, j: (ctx0 + b, 0)),
                  pl.BlockSpec((CTX_LEN // KT, KV_W, KT), lambda b, j: (ctx0 + b, 0, 0)),
                  pl.BlockSpec((SEQ, KV_W), lambda b, j: (b, 0)),
                  pl.BlockSpec((SEQ // KT, KV_W, KT), lambda b, j: (b, 0, 0)),
                  up_spec((None,)), down_spec((None,))],
        out_specs=[pl.BlockSpec((tq, ATT_W), lambda b, j: (b * nq + j, 0)), up_spec(()), down_spec(())],
        out_shape=[jax.ShapeDtypeStruct((N_TOK, ATT_W), BF16),
                   jax.ShapeDtypeStruct((D_MODEL, D_FF), BF16),
                   jax.ShapeDtypeStruct((D_FF, D_MODEL), BF16)],
        scratch_shapes=_attn_scratch(),
        compiler_params=_params(2),
        name="attn_latent",
    )(qt, k, vt, k, vt, w_up, w_down)


def _attn_ctx(qt, k, vt, att):
    ctx0 = N_LAT // CTX_LEN
    return pl.pallas_call(
        _attn_ctx_kernel,
        grid=(BATCH,),
        in_specs=[pl.BlockSpec((ATT_W, TQ), lambda b: (0, ctx0 + b)),
                  pl.BlockSpec((CTX_LEN, KV_W), lambda b: (ctx0 + b, 0)),
                  pl.BlockSpec((CTX_LEN // KT, KV_W, KT), lambda b: (ctx0 + b, 0, 0)),
                  pl.BlockSpec(memory_space=pl.ANY)],
        out_specs=pl.BlockSpec((TQ, ATT_W), lambda b: (ctx0 + b, 0)),
        out_shape=jax.ShapeDtypeStruct((N_TOK, ATT_W), BF16),
        input_output_aliases={3: 0},
        scratch_shapes=_attn_scratch(),
        compiler_params=_params(1),
        name="attn_context",
    )(qt, k, vt, att)


def _four_lat_kernel(zre0_ref, zre1_ref, zim0_ref, zim1_ref, cs_ref, tc_ref, ts_ref, o_ref,
                     are0_ref, are1_ref, aim0_ref, aim1_ref):
    cs = cs_ref[...]
    both = lambda lo_ref, hi_ref, idx: jnp.concatenate([lo_ref[idx, :], hi_ref[idx, :]], axis=1)
    for c in range(DFT_R):
        strided = pl.ds(c, DFT_R, stride=Z_PITCH)
        pr = jnp.dot(cs, both(zre0_ref, zre1_ref, strided).astype(BF16), preferred_element_type=F32)
        pi = jnp.dot(cs, both(zim0_ref, zim1_ref, strided).astype(BF16), preferred_element_type=F32)
        are = pr[:DFT_R] + pi[DFT_R:]
        aim = pi[:DFT_R] - pr[DFT_R:]
        are0_ref[strided, :] = are[:, :LANES]
        are1_ref[strided, :] = are[:, LANES:]
        aim0_ref[strided, :] = aim[:, :LANES]
        aim1_ref[strided, :] = aim[:, LANES:]

    def group(lo_ref, hi_ref, m):
        parts = [both(lo_ref, hi_ref, pl.ds((m * F2_GROUP + g) * Z_PITCH, DFT_R)) for g in range(F2_GROUP)]
        return jnp.concatenate(parts, axis=0).astype(BF16)

    for m in range(DFT_R // F2_GROUP):
        y = (jnp.dot(tc_ref[m], group(are0_ref, are1_ref, m), preferred_element_type=F32)
             + jnp.dot(ts_ref[m], group(aim0_ref, aim1_ref, m), preferred_element_type=F32))
        o_ref[:, m, :, :] = y.reshape(DFT_R, F2_GROUP, FOUR_W)


def _four_lat(zre, zim, cs64, tc, ts):
    n_groups = DFT_R // F2_GROUP
    rows = F2_GROUP * DFT_R
    seq_rows = DFT_R * Z_PITCH
    lo = pl.BlockSpec((seq_rows, LANES), lambda b: (b, 0))
    hi = pl.BlockSpec((seq_rows, LANES), lambda b: (b, 1))
    t_spec = pl.BlockSpec((n_groups, rows, rows), lambda b: (0, 0, 0), pipeline_mode=pl.Buffered(1))
    yf = pl.pallas_call(
        _four_lat_kernel,
        grid=(BATCH,),
        in_specs=[lo, hi, lo, hi, _full((2 * DFT_R, DFT_R)), t_spec, t_spec],
        out_specs=pl.BlockSpec((DFT_R, n_groups, F2_GROUP, FOUR_W), lambda b: (b, 0, 0, 0)),
        out_shape=jax.ShapeDtypeStruct((N_TOK // DFT_R, n_groups, F2_GROUP, FOUR_W), F32),
        scratch_shapes=[pltpu.VMEM((seq_rows, LANES), F32)] * 4,
        compiler_params=_params(1),
        name="fourier_latent",
    )(zre, zre, zim, zim, cs64, tc, ts)
    return yf.reshape(N_TOK, FOUR_W)


def _four_ctx_kernel(zre_ref, zim_ref, c_ref, s_ref, prev_ref, o_ref):
    del prev_ref
    rows = lambda ref: jnp.concatenate(
        [ref[pl.ds(r * Z_PITCH, DFT_R), :] for r in range(CTX_LEN // DFT_R)], axis=0).astype(BF16)
    o_ref[...] = (jnp.dot(c_ref[...], rows(zre_ref), preferred_element_type=F32)
                  + jnp.dot(s_ref[...], rows(zim_ref), preferred_element_type=F32))


def _four_ctx(zre, zim, c256, s256, yf):
    ctx0 = N_LAT // CTX_LEN
    z_spec = pl.BlockSpec((CTX_LEN // DFT_R * Z_PITCH, FOUR_W), lambda b: (ctx0 + b, 0))
    blk = pl.BlockSpec((CTX_LEN, FOUR_W), lambda b: (ctx0 + b, 0))
    return pl.pallas_call(
        _four_ctx_kernel,
        grid=(BATCH,),
        in_specs=[z_spec, z_spec, _full((CTX_LEN, CTX_LEN)), _full((CTX_LEN, CTX_LEN)),
                  pl.BlockSpec(memory_space=pl.ANY)],
        out_specs=blk,
        out_shape=jax.ShapeDtypeStruct((N_TOK, FOUR_W), F32),
        input_output_aliases={4: 0},
        compiler_params=_params(1),
        name="fourier_context",
    )(zre, zim, c256, s256, yf)


def _outproj_kernel(*refs, n_x):
    ubc_ref, hp_ref, hn_ref, yf_ref, att_ref = refs[:5]
    mod_ref, g_ref, cw_ref, wo_ref, o_ref = refs[5 + n_x:]
    read_x = _token_rows(refs[5:5 + n_x])
    i = pl.program_id(0)
    ubc = ubc_ref[...].astype(F32)
    u, bg, cg = ubc[:, :CONV_W], ubc[:, CONV_W:2 * CONV_W], ubc[:, 2 * CONV_W:]
    z = cg * u
    hp = hp_ref[HALO - 1:HALO, :].astype(F32)
    hn = hn_ref[0:1, :].astype(F32)
    z_before = hp[:, 2 * CONV_W:] * hp[:, :CONV_W]
    z_after = hn[:, 2 * CONV_W:] * hn[:, :CONV_W]
    row = lax.broadcasted_iota(jnp.int32, (SUB, 1), 0)
    w_mix = jnp.concatenate([wo_ref[CONV_W + FOUR_W:, :], wo_ref[CONV_W:CONV_W + FOUR_W, :],
                             wo_ref[:CONV_W, :]], axis=0)
    gate_gain = mod_ref[2:3, :] * g_ref[...]
    for sb in range(TM // SUB):
        rows = slice(sb * SUB, (sb + 1) * SUB)
        tok = i * TM + sb * SUB
        seg = jnp.where(tok < N_LAT, SEQ, CTX_LEN)
        starts = jnp.bitwise_and(tok, seg - 1) == 0
        ends = jnp.bitwise_and(tok + SUB, seg - 1) == 0
        above = z_before if sb == 0 else z[sb * SUB - 1:sb * SUB]
        below = z_after if (sb + 1) * SUB == TM else z[(sb + 1) * SUB:(sb + 1) * SUB + 1]
        above = jnp.where(starts, 0.0, above)
        below = jnp.where(ends, 0.0, below)
        zs = z[rows]
        z_prev = jnp.where(row == 0, above, pltpu.roll(zs, 1, 0))
        z_next = jnp.where(row == SUB - 1, below, pltpu.roll(zs, SUB - 1, 0))
        conv = bg[rows] * (z_prev * cw_ref[0:1, :] + zs * cw_ref[1:2, :] + z_next * cw_ref[2:3, :])
        mix = jnp.concatenate([att_ref[rows, :], yf_ref[rows, :].astype(BF16), conv.astype(BF16)], axis=1)
        y = jnp.dot(mix, w_mix, preferred_element_type=F32)
        scale = lax.rsqrt(jnp.mean(y * y, axis=-1, keepdims=True) + EPS)
        o_ref[rows, :] = read_x(rows) + y * scale * gate_gain


def _outproj(l, n_rows, ubc, yf, att, xs, mod, g, cw, wo):
    n_tiles = n_rows // TM
    per = TM // HALO
    last_halo = n_rows // HALO - 1
    row = lambda i: (i, 0)
    return pl.pallas_call(
        functools.partial(_outproj_kernel, n_x=len(xs)),
        grid=(n_tiles,),
        in_specs=[pl.BlockSpec((TM, 3 * CONV_W), row),
                  pl.BlockSpec((HALO, 3 * CONV_W), lambda i: (jnp.maximum(i * per - 1, 0), 0)),
                  pl.BlockSpec((HALO, 3 * CONV_W), lambda i: (jnp.minimum((i + 1) * per, last_halo), 0)),
                  pl.BlockSpec((TM, FOUR_W), row),
                  pl.BlockSpec((TM, ATT_W), row)] + _token_specs(len(xs) == 2) + [
                  _mod_spec(l, TM),
                  _layer(l, (1, D_MODEL)),
                  _layer(l, (3, CONV_W)),
                  _layer(l, (MIX_W_ROWS, D_MODEL))],
        out_specs=pl.BlockSpec((TM, D_MODEL), row),
        out_shape=jax.ShapeDtypeStruct((n_rows, D_MODEL), F32),
        compiler_params=_params(1),
        name="out_proj",
    )(ubc, ubc, ubc, yf, att, *xs, mod, g, cw, wo)


FF_CHUNK = 1024
FF_LEAD = 1


def _ffn_kernel(x_ref, mod_ref, gpre_ref, gpost_ref, wu_ref, wd_ref, o_ref):
    n_chunks = D_FF // FF_CHUNK
    work = [(sb, c) for sb in range(TM_FF // SUB) for c in range(n_chunks)]

    normed = {}

    def up(sb, c):
        if sb not in normed:
            x = x_ref[sb * SUB:(sb + 1) * SUB, :]
            normed[sb] = (_rms(x, gpre_ref[...]) * (1.0 + mod_ref[4:5, :]) + mod_ref[3:4, :]).astype(BF16)
        return jnp.dot(normed[sb], wu_ref[:, c * FF_CHUNK:(c + 1) * FF_CHUNK], preferred_element_type=F32)

    def down(c, a):
        a = jnp.maximum(a, 0.0)
        return jnp.dot((a * a).astype(BF16), wd_ref[c * FF_CHUNK:(c + 1) * FF_CHUNK, :],
                       preferred_element_type=F32)

    pending = {}
    acc = None
    for step in range(len(work) + FF_LEAD):
        if step < len(work):
            pending[step] = up(*work[step])
        if step >= FF_LEAD:
            sb, c = work[step - FF_LEAD]
            d = down(c, pending.pop(step - FF_LEAD))
            acc = d if c == 0 else acc + d
            if c == n_chunks - 1:
                rows = slice(sb * SUB, (sb + 1) * SUB)
                o_ref[rows, :] = x_ref[rows, :] + mod_ref[5:6, :] * _rms(acc, gpost_ref[...])


def _ffn(l, n_rows, x, mod, gpre, gpost, wu, wd):
    row = lambda i: (i, 0)
    return pl.pallas_call(
        _ffn_kernel,
        grid=(n_rows // TM_FF,),
        in_specs=[pl.BlockSpec((TM_FF, D_MODEL), row),
                  _mod_spec(l, TM_FF),
                  _layer(l, (1, D_MODEL)),
                  _layer(l, (1, D_MODEL)),
                  pl.BlockSpec((D_MODEL, D_FF), lambda i: (0, 0), pipeline_mode=pl.Buffered(1)),
                  pl.BlockSpec((D_FF, D_MODEL), lambda i: (0, 0), pipeline_mode=pl.Buffered(1))],
        out_specs=pl.BlockSpec((TM_FF, D_MODEL), row),
        out_shape=jax.ShapeDtypeStruct((n_rows, D_MODEL), F32),
        compiler_params=_params(1),
        name="ffn",
    )(x, mod, gpre, gpost, wu, wd)


def _rope_tables():
    t = jnp.arange(SEQ)
    rowp = (t // GRID_W).astype(F32)
    colp = (t % GRID_W).astype(F32)
    inv = 1.0 / (ROPE_THETA ** (jnp.arange(0, AXIS_DIM, 2, dtype=F32) / AXIS_DIM))
    ang = jnp.stack([rowp[:, None] * inv, colp[:, None] * inv], axis=1)
    ang = jnp.broadcast_to(ang[:, :, None, :], (SEQ, 2, 2, AXIS_DIM // 2)).reshape(SEQ, HEAD_DIM)
    cos, sin = jnp.cos(ang), jnp.sin(ang)
    first_half = (jnp.arange(HEAD_DIM) % AXIS_DIM) < AXIS_DIM // 2
    sin_lo = jnp.where(first_half, -sin, 0.0)
    sin_hi = jnp.where(first_half, 0.0, sin)
    ident = lambda v: jnp.full((TM, HEAD_DIM), v, F32)
    wide = lambda a, v: jnp.tile(jnp.concatenate([a, ident(v)], axis=0), (1, LANES // HEAD_DIM))
    return wide(cos, 1.0), wide(sin_lo, 0.0), wide(sin_hi, 0.0)


def _dft_tables():
    two_pi = 2.0 * np.pi
    n = np.arange(FOUR_GROUP_DIM)
    ang = two_pi * np.outer(n, n) / FOUR_GROUP_DIM
    eye = np.eye(FOUR_GROUPS)
    scale = FOUR_GROUP_DIM ** -0.5
    mch = np.concatenate([np.kron(eye, np.cos(ang)), -np.kron(eye, np.sin(ang))], axis=1) * scale
    r = np.arange(DFT_R)
    ang_r = two_pi * np.outer(r, r) / DFT_R
    cs64 = np.concatenate([np.cos(ang_r), np.sin(ang_r)], axis=0)
    n_groups = DFT_R // F2_GROUP
    m = np.arange(n_groups)[:, None, None, None]
    k2 = np.arange(DFT_R)[None, :, None, None]
    g = np.arange(F2_GROUP)[None, None, :, None]
    c = np.arange(DFT_R)[None, None, None, :]
    k = F2_GROUP * m + g + DFT_R * k2
    ang2 = two_pi * ((k * c) % SEQ) / SEQ
    sel = np.eye(F2_GROUP)[None, None, :, :, None]
    norm2 = SEQ ** -0.5
    tc = (np.cos(ang2)[:, :, :, None, :] * sel * norm2).reshape(n_groups, DFT_R * F2_GROUP, F2_GROUP * DFT_R)
    ts = (np.sin(ang2)[:, :, :, None, :] * sel * norm2).reshape(n_groups, DFT_R * F2_GROUP, F2_GROUP * DFT_R)
    t = np.arange(CTX_LEN)
    ang_c = two_pi * (np.outer(t, t) % CTX_LEN) / CTX_LEN
    normc = CTX_LEN ** -0.5
    c256, s256 = np.cos(ang_c) * normc, np.sin(ang_c) * normc
    as_bf16 = lambda a: jnp.asarray(a, dtype=F32).astype(BF16)
    return tuple(as_bf16(a) for a in (mch, cs64, tc, ts, c256, s256))


def _head_mean_matrix():
    blocks = np.kron(np.eye(N_HEADS), np.full((HEAD_DIM, HEAD_DIM), 1.0 / HEAD_DIM))
    return jnp.asarray(blocks, dtype=F32).astype(BF16)


def kernel(x, c, ctx, c_ctx, w_ada, b_ada, w_in, conv_w, q_gain, k_gain, w_out,
           g_pre_mix, g_post_mix, g_pre_ffn, g_post_ffn, w_ffn_up, w_ffn_down):
    cos, sin_lo, sin_hi = _rope_tables()
    mch, cs64, tc, ts, c256, s256 = _dft_tables()
    ones = _head_mean_matrix()

    c8 = jnp.concatenate([c, c_ctx[None, :], jnp.zeros((MOD_ROWS - BATCH - 1, D_MODEL), F32)], axis=0)
    mod = _ada(c8, w_ada, b_ada).reshape(DEPTH, MOD_ROWS, N_MOD, D_MODEL)

    xs = (x.reshape(N_LAT, D_MODEL), ctx.reshape(N_CTX, D_MODEL))
    w_in_b, w_out_b = w_in.astype(BF16), w_out.astype(BF16)
    rows = lambda a: a.reshape(DEPTH, 1, -1)
    qg, kg = rows(jnp.tile(q_gain, (1, N_HEADS))), rows(jnp.tile(k_gain, (1, N_KV)))
    g_pre_mix, g_post_mix, g_pre_ffn, g_post_ffn = map(rows, (g_pre_mix, g_post_mix, g_pre_ffn, g_post_ffn))

    for l in range(DEPTH):
        last = l == DEPTH - 1
        ubc, zre, zim, qt, k, vt = _inproj(l, xs, mod, g_pre_mix, w_in_b, mch, ones,
                                           qg, kg, cos, sin_lo, sin_hi)
        att, w_up_b, w_down_b = _attn_lat(l, qt, k, vt, w_ffn_up, w_ffn_down)
        yf = _four_lat(zre, zim, cs64, tc, ts)
        if last:
            n_rows = N_LAT
        else:
            n_rows = N_TOK
            att = _attn_ctx(qt, k, vt, att)
            yf = _four_ctx(zre, zim, c256, s256, yf)
        x_mid = _outproj(l, n_rows, ubc, yf, att, xs, mod, g_post_mix, conv_w, w_out_b)
        xs = (_ffn(l, n_rows, x_mid, mod, g_pre_ffn, g_post_ffn, w_up_b, w_down_b),)
    return xs[0].reshape(BATCH, SEQ, D_MODEL)
```

```python
import functools
import math

import jax
import jax.numpy as jnp
import numpy as np
from jax import lax
from jax.experimental import pallas as pl
from jax.experimental.pallas import tpu as pltpu

F32 = jnp.float32
BF16 = jnp.bfloat16

D_MODEL = 1024
BATCH = 4
SEQ = 4096
DEPTH = 4
GRID_W = 64
CTX_LEN = 256
CONV_W = 256
FOUR_GROUPS = 4
FOUR_GROUP_DIM = 64
FOUR_W = FOUR_GROUPS * FOUR_GROUP_DIM
N_HEADS = 8
N_KV = 2
GROUP = N_HEADS // N_KV
HEAD_DIM = 64
ATT_W = N_HEADS * HEAD_DIM
KV_W = N_KV * HEAD_DIM
Q_START = 3 * CONV_W + FOUR_W
K_START = Q_START + ATT_W
V_START = K_START + KV_W
PROJ_W = V_START + KV_W
AXIS_DIM = HEAD_DIM // 2
ROPE_THETA = 10000.0
D_FF = 4 * D_MODEL
N_MOD = 6
EPS = 1e-6

N_LAT = BATCH * SEQ
N_CTX = BATCH * CTX_LEN
N_TOK = N_LAT + N_CTX
MOD_ROWS = 8
CTX_MOD_ROW = BATCH

LANES = 128
BF16_SUBLANES = 16
VMEM_LIMIT_BYTES = 56 * 1024 * 1024

TM = 1024
SUB = 256
ADA_TN = 1024
TQ = 256
KT = 256
Q_PER_STEP = 2
SCORE_LEAD = 6
LOG2_E = math.log2(math.e)
MIX_W_ROWS = CONV_W + FOUR_W + ATT_W
DFT_R = 64
F2_GROUP = 8
HALO = BF16_SUBLANES
Z_PITCH = DFT_R + 8
Z_ROWS = N_TOK // DFT_R * Z_PITCH

assert SEQ == DFT_R * DFT_R and SEQ % TM == 0 and N_CTX % TM == 0
assert CTX_LEN == TQ == KT == SUB and SEQ % (TQ * Q_PER_STEP) == 0 and TM % SUB == 0


def _params(n_axes):
    return pltpu.CompilerParams(dimension_semantics=("arbitrary",) * n_axes,
                                vmem_limit_bytes=VMEM_LIMIT_BYTES)


def _full(shape):
    return pl.BlockSpec(shape, lambda *_: (0,) * len(shape))


def _layer(l, shape):
    return pl.BlockSpec((None,) + shape, lambda *_: (l,) + (0,) * len(shape))


def _mod_spec(l, tm):
    def index(i):
        return (l, jnp.where(i < N_LAT // tm, i // (SEQ // tm), CTX_MOD_ROW), 0, 0)
    return pl.BlockSpec((None, None, N_MOD, D_MODEL), index)


def _ada_kernel(c_ref, w_ref, b_ref, o_ref):
    c = c_ref[...]
    s = c * jax.nn.sigmoid(c)
    split = lambda a: (a.astype(BF16), (a - a.astype(BF16).astype(F32)).astype(BF16))
    s_hi, s_lo = split(s)
    w_hi, w_lo = split(w_ref[...])
    head = jnp.dot(jnp.concatenate([s_hi, s_lo], axis=0), w_hi, preferred_element_type=F32)
    tail = jnp.dot(s_hi, w_lo, preferred_element_type=F32)
    o_ref[...] = head[:MOD_ROWS] + head[MOD_ROWS:] + tail + b_ref[...]


def _ada(c8, w_ada, b_ada):
    tn = ADA_TN
    n = N_MOD * D_MODEL
    return pl.pallas_call(
        _ada_kernel,
        grid=(DEPTH, n // tn),
        in_specs=[pl.BlockSpec((MOD_ROWS, D_MODEL), lambda l, j: (0, 0)),
                  pl.BlockSpec((None, D_MODEL, tn), lambda l, j: (l, 0, j)),
                  pl.BlockSpec((None, 1, tn), lambda l, j: (l, 0, j))],
        out_specs=pl.BlockSpec((None, MOD_ROWS, tn), lambda l, j: (l, 0, j)),
        out_shape=jax.ShapeDtypeStruct((DEPTH, MOD_ROWS, n), F32),
        compiler_params=_params(2),
        name="ada",
    )(c8, w_ada, b_ada.reshape(DEPTH, 1, n))


def _rms(x, g):
    return x * lax.rsqrt(jnp.mean(x * x, axis=-1, keepdims=True) + EPS) * g


def _head_norm_rope(t, ones, gain, cos, sin_lo, sin_hi):
    w = t.shape[-1]
    ms = jnp.dot((t * t).astype(BF16), ones, preferred_element_type=F32)
    tn = t * lax.rsqrt(ms + EPS) * gain
    up = pltpu.roll(tn, w - AXIS_DIM // 2, 1)
    dn = pltpu.roll(tn, AXIS_DIM // 2, 1)
    return tn * cos + up * sin_lo + dn * sin_hi


def _token_rows(x_refs):
    if len(x_refs) == 1:
        return lambda rows: x_refs[0][rows, :]
    lat_ref, ctx_ref = x_refs
    is_lat = pl.program_id(0) < N_LAT // TM
    return lambda rows: jnp.where(is_lat, lat_ref[rows, :], ctx_ref[rows, :])


def _token_specs(split):
    row = lambda i: (i, 0)
    if not split:
        return [pl.BlockSpec((TM, D_MODEL), row)]
    n_lat_tiles = N_LAT // TM
    return [pl.BlockSpec((TM, D_MODEL), lambda i: (jnp.minimum(i, n_lat_tiles - 1), 0)),
            pl.BlockSpec((TM, D_MODEL), lambda i: (jnp.maximum(i - n_lat_tiles, 0), 0))]


def _inproj_kernel(*refs, n_x):
    (mod_ref, g_ref, w_ref, mch_ref, ones_ref, qg_ref, kg_ref, cos_ref, slo_ref, shi_ref,
     ubc_ref, zre_ref, zim_ref, qt_ref, k_ref, vt_ref) = refs[n_x:]
    read_x = _token_rows(refs[:n_x])
    rep = ATT_W // LANES
    n_sub = TM // SUB

    def project(sb):
        rows = slice(sb * SUB, (sb + 1) * SUB)
        h = _rms(read_x(rows), g_ref[...]) * (1.0 + mod_ref[1:2, :]) + mod_ref[0:1, :]
        return jnp.dot(h.astype(BF16), w_ref[...], preferred_element_type=F32)

    def finish(sb, p):
        rows = slice(sb * SUB, (sb + 1) * SUB)
        ubc_ref[rows, :] = p[:, :3 * CONV_W].astype(BF16)
        z = jnp.dot(p[:, 3 * CONV_W:Q_START].astype(BF16), mch_ref[...], preferred_element_type=F32)
        for rr in range(SUB // DFT_R):
            src = slice(rr * DFT_R, (rr + 1) * DFT_R)
            dst = pl.ds((sb * (SUB // DFT_R) + rr) * Z_PITCH, DFT_R)
            zre_ref[dst, :] = z[src, :FOUR_W]
            zim_ref[dst, :] = z[src, FOUR_W:]
        cos, slo, shi = cos_ref[rows, :], slo_ref[rows, :], shi_ref[rows, :]
        q = _head_norm_rope(p[:, Q_START:K_START], ones_ref[...], qg_ref[...],
                            jnp.tile(cos, (1, rep)), jnp.tile(slo, (1, rep)), jnp.tile(shi, (1, rep)))
        qt_ref[:, rows] = (q * (HEAD_DIM ** -0.5 * LOG2_E)).T.astype(BF16)
        k = _head_norm_rope(p[:, K_START:V_START], ones_ref[:KV_W, :KV_W], kg_ref[...], cos, slo, shi)
        k_ref[rows, :] = k.astype(BF16)
        vt_ref[sb] = p[:, V_START:].T.astype(BF16)

    for sb in range(n_sub):
        finish(sb, project(sb))


def _inproj(l, xs, mod, g, w, mch, ones, qg, kg, cos, slo, shi):
    n_tiles = N_TOK // TM
    n_lat_tiles = N_LAT // TM
    pos_tiles = SEQ // TM

    def pos_map(i):
        return (jnp.where(i < n_lat_tiles, i % pos_tiles, pos_tiles), 0)

    row = lambda i: (i, 0)
    pos_spec = pl.BlockSpec((TM, LANES), pos_map)
    return pl.pallas_call(
        functools.partial(_inproj_kernel, n_x=len(xs)),
        grid=(n_tiles,),
        in_specs=_token_specs(len(xs) == 2) + [
                  _mod_spec(l, TM),
                  _layer(l, (1, D_MODEL)),
                  _full((D_MODEL, PROJ_W)),
                  _full((FOUR_W, 2 * FOUR_W)),
                  _full((ATT_W, ATT_W)),
                  _layer(l, (1, ATT_W)),
                  _layer(l, (1, KV_W)),
                  pos_spec, pos_spec, pos_spec],
        out_specs=[pl.BlockSpec((TM, 3 * CONV_W), row),
                   pl.BlockSpec((TM // DFT_R * Z_PITCH, FOUR_W), row),
                   pl.BlockSpec((TM // DFT_R * Z_PITCH, FOUR_W), row),
                   pl.BlockSpec((ATT_W, TM), lambda i: (0, i)),
                   pl.BlockSpec((TM, KV_W), row),
                   pl.BlockSpec((TM // KT, KV_W, KT), lambda i: (i, 0, 0))],
        out_shape=[jax.ShapeDtypeStruct((N_TOK, 3 * CONV_W), BF16),
                   jax.ShapeDtypeStruct((Z_ROWS, FOUR_W), F32),
                   jax.ShapeDtypeStruct((Z_ROWS, FOUR_W), F32),
                   jax.ShapeDtypeStruct((ATT_W, N_TOK), BF16),
                   jax.ShapeDtypeStruct((N_TOK, KV_W), BF16),
                   jax.ShapeDtypeStruct((N_TOK // KT, KV_W, KT), BF16)],
        compiler_params=_params(1),
        name="in_proj",
    )(*xs, mod, g, w, mch, ones, qg, kg, cos, slo, shi)


def _attn_body(qt_ref, kc_ref, vtc_ref, kl_ref, vtl_ref, o_ref, ot_ref, *, n_lat_tiles):
    sub = 8
    ones = jnp.ones((BF16_SUBLANES, KT), BF16)

    n_tiles = 1 + n_lat_tiles
    n_q = qt_ref.shape[1] // TQ
    work = [(qi, g, j) for qi in range(n_q) for g in range(N_HEADS) for j in range(n_tiles)]

    def scores(qi, g, j):
        qt = qt_ref[g * HEAD_DIM:(g + 1) * HEAD_DIM, qi * TQ:(qi + 1) * TQ]
        zero = jnp.zeros_like(qt)
        w = jnp.concatenate([qt, zero] if g // GROUP == 0 else [zero, qt], axis=0)
        k_tile = kc_ref[...] if j == 0 else kl_ref[(j - 1) * KT:j * KT, :]
        return jnp.dot(k_tile, w, preferred_element_type=F32)

    def accumulate(g, j, s, m, acc):
        hv = g // GROUP
        v_rows = slice(hv * HEAD_DIM, (hv + 1) * HEAD_DIM)
        vt_tile = vtc_ref[0, v_rows, :] if j == 0 else vtl_ref[j - 1, v_rows, :]
        tile_max = jnp.max(jnp.max(s.reshape(KT // sub, sub, TQ), axis=0), axis=0, keepdims=True)
        m_new = tile_max if j == 0 else jnp.maximum(m, tile_max)
        p = jnp.exp2(s - m_new).astype(BF16)
        lhs = jnp.concatenate([vt_tile, ones], axis=0)
        pv = jnp.dot(lhs, p, preferred_element_type=F32)
        if j > 0:
            pv = pv + acc * jnp.exp2(m - m_new)
        return m_new, pv

    pending = {}
    m = acc = None
    for step in range(len(work) + SCORE_LEAD):
        if step < len(work):
            pending[step] = scores(*work[step])
        if step >= SCORE_LEAD:
            qi, g, j = work[step - SCORE_LEAD]
            m, acc = accumulate(g, j, pending.pop(step - SCORE_LEAD), m, acc)
            if j == n_tiles - 1:
                ot_ref[g * HEAD_DIM:(g + 1) * HEAD_DIM, :] = acc[:HEAD_DIM] / acc[HEAD_DIM:HEAD_DIM + 1]
                if g == N_HEADS - 1:
                    o_ref[qi * TQ:(qi + 1) * TQ, :] = ot_ref[...].T.astype(o_ref.dtype)


def _attn_lat_kernel(*refs, n_cast):
    qt_ref, kc_ref, vtc_ref, kl_ref, vtl_ref = refs[:5]
    w_refs = refs[5:5 + n_cast]
    o_ref = refs[5 + n_cast]
    w_out_refs = refs[6 + n_cast:6 + 2 * n_cast]
    ot_ref = refs[6 + 2 * n_cast]
    _attn_body(qt_ref, kc_ref, vtc_ref, kl_ref, vtl_ref, o_ref, ot_ref, n_lat_tiles=SEQ // KT)
    for w_ref, w_out_ref in zip(w_refs, w_out_refs):
        w_out_ref[...] = w_ref[...].astype(BF16)


def _attn_ctx_kernel(qt_ref, kc_ref, vtc_ref, prev_ref, o_ref, ot_ref):
    del prev_ref
    _attn_body(qt_ref, kc_ref, vtc_ref, None, None, o_ref, ot_ref, n_lat_tiles=0)


def _attn_scratch():
    return [pltpu.VMEM((ATT_W, TQ), F32)]


def _attn_lat(qt, k, vt, casts):
    tq = TQ * Q_PER_STEP
    nq = SEQ // tq
    n_steps = BATCH * nq
    ctx0 = N_LAT // CTX_LEN
    cast_in, cast_out, cast_shape = [], [], []
    for w, layer in casts:
        n_rows, n_cols = w.shape[1:]
        blk = (n_rows // n_steps, n_cols)
        cast_in.append(pl.BlockSpec((None,) + blk, lambda b, j, layer=layer: (layer, b * nq + j, 0)))
        cast_out.append(pl.BlockSpec(blk, lambda b, j: (b * nq + j, 0)))
        cast_shape.append(jax.ShapeDtypeStruct((n_rows, n_cols), BF16))
    return pl.pallas_call(
        functools.partial(_attn_lat_kernel, n_cast=len(casts)),
        grid=(BATCH, nq),
        in_specs=[pl.BlockSpec((ATT_W, tq), lambda b, j: (0, b * nq + j)),
                  pl.BlockSpec((CTX_LEN, KV_W), lambda b, j: (ctx0 + b, 0)),
                  pl.BlockSpec((CTX_LEN // KT, KV_W, KT), lambda b, j: (ctx0 + b, 0, 0)),
                  pl.BlockSpec((SEQ, KV_W), lambda b, j: (b, 0)),
                  pl.BlockSpec((SEQ // KT, KV_W, KT), lambda b, j: (b, 0, 0))] + cast_in,
        out_specs=[pl.BlockSpec((tq, ATT_W), lambda b, j: (b * nq + j, 0))] + cast_out,
        out_shape=[jax.ShapeDtypeStruct((N_TOK, ATT_W), BF16)] + cast_shape,
        scratch_shapes=_attn_scratch(),
        compiler_params=_params(2),
        name="attn_latent",
    )(qt, k, vt, k, vt, *[w for w, _ in casts])


def _attn_ctx(qt, k, vt, att):
    ctx0 = N_LAT // CTX_LEN
    return pl.pallas_call(
        _attn_ctx_kernel,
        grid=(BATCH,),
        in_specs=[pl.BlockSpec((ATT_W, TQ), lambda b: (0, ctx0 + b)),
                  pl.BlockSpec((CTX_LEN, KV_W), lambda b: (ctx0 + b, 0)),
                  pl.BlockSpec((CTX_LEN // KT, KV_W, KT), lambda b: (ctx0 + b, 0, 0)),
                  pl.BlockSpec(memory_space=pl.ANY)],
        out_specs=pl.BlockSpec((TQ, ATT_W), lambda b: (ctx0 + b, 0)),
        out_shape=jax.ShapeDtypeStruct((N_TOK, ATT_W), BF16),
        input_output_aliases={3: 0},
        scratch_shapes=_attn_scratch(),
        compiler_params=_params(1),
        name="attn_context",
    )(qt, k, vt, att)


def _four_lat_kernel(zre0_ref, zre1_ref, zim0_ref, zim1_ref, cs_ref, tc_ref, ts_ref, o_ref,
                     are0_ref, are1_ref, aim0_ref, aim1_ref):
    cs = cs_ref[...]
    both = lambda lo_ref, hi_ref, idx: jnp.concatenate([lo_ref[idx, :], hi_ref[idx, :]], axis=1)
    for c in range(DFT_R):
        strided = pl.ds(c, DFT_R, stride=Z_PITCH)
        pr = jnp.dot(cs, both(zre0_ref, zre1_ref, strided).astype(BF16), preferred_element_type=F32)
        pi = jnp.dot(cs, both(zim0_ref, zim1_ref, strided).astype(BF16), preferred_element_type=F32)
        are = pr[:DFT_R] + pi[DFT_R:]
        aim = pi[:DFT_R] - pr[DFT_R:]
        are0_ref[strided, :] = are[:, :LANES]
        are1_ref[strided, :] = are[:, LANES:]
        aim0_ref[strided, :] = aim[:, :LANES]
        aim1_ref[strided, :] = aim[:, LANES:]

    def group(lo_ref, hi_ref, m):
        parts = [both(lo_ref, hi_ref, pl.ds((m * F2_GROUP + g) * Z_PITCH, DFT_R)) for g in range(F2_GROUP)]
        return jnp.concatenate(parts, axis=0).astype(BF16)

    for m in range(DFT_R // F2_GROUP):
        y = (jnp.dot(tc_ref[m], group(are0_ref, are1_ref, m), preferred_element_type=F32)
             + jnp.dot(ts_ref[m], group(aim0_ref, aim1_ref, m), preferred_element_type=F32))
        o_ref[:, m, :, :] = y.reshape(DFT_R, F2_GROUP, FOUR_W)


def _four_lat(zre, zim, cs64, tc, ts):
    n_groups = DFT_R // F2_GROUP
    rows = F2_GROUP * DFT_R
    seq_rows = DFT_R * Z_PITCH
    lo = pl.BlockSpec((seq_rows, LANES), lambda b: (b, 0))
    hi = pl.BlockSpec((seq_rows, LANES), lambda b: (b, 1))
    t_spec = pl.BlockSpec((n_groups, rows, rows), lambda b: (0, 0, 0), pipeline_mode=pl.Buffered(1))
    yf = pl.pallas_call(
        _four_lat_kernel,
        grid=(BATCH,),
        in_specs=[lo, hi, lo, hi, _full((2 * DFT_R, DFT_R)), t_spec, t_spec],
        out_specs=pl.BlockSpec((DFT_R, n_groups, F2_GROUP, FOUR_W), lambda b: (b, 0, 0, 0)),
        out_shape=jax.ShapeDtypeStruct((N_TOK // DFT_R, n_groups, F2_GROUP, FOUR_W), F32),
        scratch_shapes=[pltpu.VMEM((seq_rows, LANES), F32)] * 4,
        compiler_params=_params(1),
        name="fourier_latent",
    )(zre, zre, zim, zim, cs64, tc, ts)
    return yf.reshape(N_TOK, FOUR_W)


def _four_ctx_kernel(zre_ref, zim_ref, c_ref, s_ref, prev_ref, o_ref):
    del prev_ref
    rows = lambda ref: jnp.concatenate(
        [ref[pl.ds(r * Z_PITCH, DFT_R), :] for r in range(CTX_LEN // DFT_R)], axis=0).astype(BF16)
    o_ref[...] = (jnp.dot(c_ref[...], rows(zre_ref), preferred_element_type=F32)
                  + jnp.dot(s_ref[...], rows(zim_ref), preferred_element_type=F32))


def _four_ctx(zre, zim, c256, s256, yf):
    ctx0 = N_LAT // CTX_LEN
    z_spec = pl.BlockSpec((CTX_LEN // DFT_R * Z_PITCH, FOUR_W), lambda b: (ctx0 + b, 0))
    blk = pl.BlockSpec((CTX_LEN, FOUR_W), lambda b: (ctx0 + b, 0))
    return pl.pallas_call(
        _four_ctx_kernel,
        grid=(BATCH,),
        in_specs=[z_spec, z_spec, _full((CTX_LEN, CTX_LEN)), _full((CTX_LEN, CTX_LEN)),
                  pl.BlockSpec(memory_space=pl.ANY)],
        out_specs=blk,
        out_shape=jax.ShapeDtypeStruct((N_TOK, FOUR_W), F32),
        input_output_aliases={4: 0},
        compiler_params=_params(1),
        name="fourier_context",
    )(zre, zim, c256, s256, yf)


def _outproj_kernel(*refs, n_x):
    ubc_ref, hp_ref, hn_ref, yf_ref, att_ref = refs[:5]
    mod_ref, g_ref, cw_ref, wo_ref, o_ref = refs[5 + n_x:]
    read_x = _token_rows(refs[5:5 + n_x])
    i = pl.program_id(0)
    ubc = ubc_ref[...].astype(F32)
    u, bg, cg = ubc[:, :CONV_W], ubc[:, CONV_W:2 * CONV_W], ubc[:, 2 * CONV_W:]
    z = cg * u
    hp = hp_ref[HALO - 1:HALO, :].astype(F32)
    hn = hn_ref[0:1, :].astype(F32)
    z_before = hp[:, 2 * CONV_W:] * hp[:, :CONV_W]
    z_after = hn[:, 2 * CONV_W:] * hn[:, :CONV_W]
    row = lax.broadcasted_iota(jnp.int32, (SUB, 1), 0)
    w_mix = jnp.concatenate([wo_ref[CONV_W + FOUR_W:, :], wo_ref[CONV_W:CONV_W + FOUR_W, :],
                             wo_ref[:CONV_W, :]], axis=0)
    gate_gain = mod_ref[2:3, :] * g_ref[...]
    for sb in range(TM // SUB):
        rows = slice(sb * SUB, (sb + 1) * SUB)
        tok = i * TM + sb * SUB
        seg = jnp.where(tok < N_LAT, SEQ, CTX_LEN)
        starts = jnp.bitwise_and(tok, seg - 1) == 0
        ends = jnp.bitwise_and(tok + SUB, seg - 1) == 0
        above = z_before if sb == 0 else z[sb * SUB - 1:sb * SUB]
        below = z_after if (sb + 1) * SUB == TM else z[(sb + 1) * SUB:(sb + 1) * SUB + 1]
        above = jnp.where(starts, 0.0, above)
        below = jnp.where(ends, 0.0, below)
        zs = z[rows]
        z_prev = jnp.where(row == 0, above, pltpu.roll(zs, 1, 0))
        z_next = jnp.where(row == SUB - 1, below, pltpu.roll(zs, SUB - 1, 0))
        conv = bg[rows] * (z_prev * cw_ref[0:1, :] + zs * cw_ref[1:2, :] + z_next * cw_ref[2:3, :])
        mix = jnp.concatenate([att_ref[rows, :], yf_ref[rows, :].astype(BF16), conv.astype(BF16)], axis=1)
        y = jnp.dot(mix, w_mix, preferred_element_type=F32)
        scale = lax.rsqrt(jnp.mean(y * y, axis=-1, keepdims=True) + EPS)
        o_ref[rows, :] = read_x(rows) + y * scale * gate_gain


def _outproj(l, n_rows, ubc, yf, att, xs, mod, g, cw, wo):
    n_tiles = n_rows // TM
    per = TM // HALO
    last_halo = n_rows // HALO - 1
    row = lambda i: (i, 0)
    return pl.pallas_call(
        functools.partial(_outproj_kernel, n_x=len(xs)),
        grid=(n_tiles,),
        in_specs=[pl.BlockSpec((TM, 3 * CONV_W), row),
                  pl.BlockSpec((HALO, 3 * CONV_W), lambda i: (jnp.maximum(i * per - 1, 0), 0)),
                  pl.BlockSpec((HALO, 3 * CONV_W), lambda i: (jnp.minimum((i + 1) * per, last_halo), 0)),
                  pl.BlockSpec((TM, FOUR_W), row),
                  pl.BlockSpec((TM, ATT_W), row)] + _token_specs(len(xs) == 2) + [
                  _mod_spec(l, TM),
                  _layer(l, (1, D_MODEL)),
                  _layer(l, (3, CONV_W)),
                  _full((MIX_W_ROWS, D_MODEL))],
        out_specs=pl.BlockSpec((TM, D_MODEL), row),
        out_shape=jax.ShapeDtypeStruct((n_rows, D_MODEL), F32),
        compiler_params=_params(1),
        name="out_proj",
    )(ubc, ubc, ubc, yf, att, *xs, mod, g, cw, wo)


FF_CHUNK = 1024
FF_LEAD = 1


def _ffn_kernel(x_ref, mod_ref, gpre_ref, gpost_ref, wu_ref, wd_ref, o_ref):
    n_chunks = D_FF // FF_CHUNK
    work = [(sb, c) for sb in range(TM // SUB) for c in range(n_chunks)]

    normed = {}

    def up(sb, c):
        if sb not in normed:
            x = x_ref[sb * SUB:(sb + 1) * SUB, :]
            normed[sb] = (_rms(x, gpre_ref[...]) * (1.0 + mod_ref[4:5, :]) + mod_ref[3:4, :]).astype(BF16)
        return jnp.dot(normed[sb], wu_ref[:, c * FF_CHUNK:(c + 1) * FF_CHUNK], preferred_element_type=F32)

    def down(c, a):
        a = jnp.maximum(a, 0.0)
        return jnp.dot((a * a).astype(BF16), wd_ref[c * FF_CHUNK:(c + 1) * FF_CHUNK, :],
                       preferred_element_type=F32)

    pending = {}
    acc = None
    for step in range(len(work) + FF_LEAD):
        if step < len(work):
            pending[step] = up(*work[step])
        if step >= FF_LEAD:
            sb, c = work[step - FF_LEAD]
            d = down(c, pending.pop(step - FF_LEAD))
            acc = d if c == 0 else acc + d
            if c == n_chunks - 1:
                rows = slice(sb * SUB, (sb + 1) * SUB)
                o_ref[rows, :] = x_ref[rows, :] + mod_ref[5:6, :] * _rms(acc, gpost_ref[...])


def _ffn(l, n_rows, x, mod, gpre, gpost, wu, wd):
    row = lambda i: (i, 0)
    return pl.pallas_call(
        _ffn_kernel,
        grid=(n_rows // TM,),
        in_specs=[pl.BlockSpec((TM, D_MODEL), row),
                  _mod_spec(l, TM),
                  _layer(l, (1, D_MODEL)),
                  _layer(l, (1, D_MODEL)),
                  pl.BlockSpec((D_MODEL, D_FF), lambda i: (0, 0), pipeline_mode=pl.Buffered(1)),
                  pl.BlockSpec((D_FF, D_MODEL), lambda i: (0, 0), pipeline_mode=pl.Buffered(1))],
        out_specs=pl.BlockSpec((TM, D_MODEL), row),
        out_shape=jax.ShapeDtypeStruct((n_rows, D_MODEL), F32),
        compiler_params=_params(1),
        name="ffn",
    )(x, mod, gpre, gpost, wu, wd)


def _rope_tables():
    t = jnp.arange(SEQ)
    rowp = (t // GRID_W).astype(F32)
    colp = (t % GRID_W).astype(F32)
    inv = 1.0 / (ROPE_THETA ** (jnp.arange(0, AXIS_DIM, 2, dtype=F32) / AXIS_DIM))
    ang = jnp.stack([rowp[:, None] * inv, colp[:, None] * inv], axis=1)
    ang = jnp.broadcast_to(ang[:, :, None, :], (SEQ, 2, 2, AXIS_DIM // 2)).reshape(SEQ, HEAD_DIM)
    cos, sin = jnp.cos(ang), jnp.sin(ang)
    first_half = (jnp.arange(HEAD_DIM) % AXIS_DIM) < AXIS_DIM // 2
    sin_lo = jnp.where(first_half, -sin, 0.0)
    sin_hi = jnp.where(first_half, 0.0, sin)
    ident = lambda v: jnp.full((TM, HEAD_DIM), v, F32)
    wide = lambda a, v: jnp.tile(jnp.concatenate([a, ident(v)], axis=0), (1, LANES // HEAD_DIM))
    return wide(cos, 1.0), wide(sin_lo, 0.0), wide(sin_hi, 0.0)


def _dft_tables():
    two_pi = 2.0 * np.pi
    n = np.arange(FOUR_GROUP_DIM)
    ang = two_pi * np.outer(n, n) / FOUR_GROUP_DIM
    eye = np.eye(FOUR_GROUPS)
    scale = FOUR_GROUP_DIM ** -0.5
    mch = np.concatenate([np.kron(eye, np.cos(ang)), -np.kron(eye, np.sin(ang))], axis=1) * scale
    r = np.arange(DFT_R)
    ang_r = two_pi * np.outer(r, r) / DFT_R
    cs64 = np.concatenate([np.cos(ang_r), np.sin(ang_r)], axis=0)
    n_groups = DFT_R // F2_GROUP
    m = np.arange(n_groups)[:, None, None, None]
    k2 = np.arange(DFT_R)[None, :, None, None]
    g = np.arange(F2_GROUP)[None, None, :, None]
    c = np.arange(DFT_R)[None, None, None, :]
    k = F2_GROUP * m + g + DFT_R * k2
    ang2 = two_pi * ((k * c) % SEQ) / SEQ
    sel = np.eye(F2_GROUP)[None, None, :, :, None]
    norm2 = SEQ ** -0.5
    tc = (np.cos(ang2)[:, :, :, None, :] * sel * norm2).reshape(n_groups, DFT_R * F2_GROUP, F2_GROUP * DFT_R)
    ts = (np.sin(ang2)[:, :, :, None, :] * sel * norm2).reshape(n_groups, DFT_R * F2_GROUP, F2_GROUP * DFT_R)
    t = np.arange(CTX_LEN)
    ang_c = two_pi * (np.outer(t, t) % CTX_LEN) / CTX_LEN
    normc = CTX_LEN ** -0.5
    c256, s256 = np.cos(ang_c) * normc, np.sin(ang_c) * normc
    as_bf16 = lambda a: jnp.asarray(a, dtype=F32).astype(BF16)
    return tuple(as_bf16(a) for a in (mch, cs64, tc, ts, c256, s256))


def _head_mean_matrix():
    blocks = np.kron(np.eye(N_HEADS), np.full((HEAD_DIM, HEAD_DIM), 1.0 / HEAD_DIM))
    return jnp.asarray(blocks, dtype=F32).astype(BF16)


def kernel(x, c, ctx, c_ctx, w_ada, b_ada, w_in, conv_w, q_gain, k_gain, w_out,
           g_pre_mix, g_post_mix, g_pre_ffn, g_post_ffn, w_ffn_up, w_ffn_down):
    cos, sin_lo, sin_hi = _rope_tables()
    mch, cs64, tc, ts, c256, s256 = _dft_tables()
    ones = _head_mean_matrix()

    c8 = jnp.concatenate([c, c_ctx[None, :], jnp.zeros((MOD_ROWS - BATCH - 1, D_MODEL), F32)], axis=0)
    mod = _ada(c8, w_ada, b_ada).reshape(DEPTH, MOD_ROWS, N_MOD, D_MODEL)

    xs = (x.reshape(N_LAT, D_MODEL), ctx.reshape(N_CTX, D_MODEL))
    w_in_b = w_in[0].astype(BF16)
    rows = lambda a: a.reshape(DEPTH, 1, -1)
    qg, kg = rows(jnp.tile(q_gain, (1, N_HEADS))), rows(jnp.tile(k_gain, (1, N_KV)))
    g_pre_mix, g_post_mix, g_pre_ffn, g_post_ffn = map(rows, (g_pre_mix, g_post_mix, g_pre_ffn, g_post_ffn))

    for l in range(DEPTH):
        last = l == DEPTH - 1
        ubc, zre, zim, qt, k, vt = _inproj(l, xs, mod, g_pre_mix, w_in_b, mch, ones,
                                           qg, kg, cos, sin_lo, sin_hi)
        casts = [(w_out, l), (w_ffn_up, l), (w_ffn_down, l)] + ([] if last else [(w_in, l + 1)])
        att, w_out_b, w_up_b, w_down_b, *w_in_next = _attn_lat(qt, k, vt, casts)
        if not last:
            w_in_b = w_in_next[0]
        yf = _four_lat(zre, zim, cs64, tc, ts)
        if last:
            n_rows = N_LAT
        else:
            n_rows = N_TOK
            att = _attn_ctx(qt, k, vt, att)
            yf = _four_ctx(zre, zim, c256, s256, yf)
        x_mid = _outproj(l, n_rows, ubc, yf, att, xs, mod, g_post_mix, conv_w, w_out_b)
        xs = (_ffn(l, n_rows, x_mid, mod, g_pre_ffn, g_post_ffn, w_up_b, w_down_b),)
    return xs[0].reshape(BATCH, SEQ, D_MODEL)
```

```python
import functools
import math

import jax
import jax.numpy as jnp
import numpy as np
from jax import lax
from jax.experimental import pallas as pl
from jax.experimental.pallas import tpu as pltpu

F32 = jnp.float32
BF16 = jnp.bfloat16

D_MODEL = 1024
BATCH = 4
SEQ = 4096
DEPTH = 4
GRID_W = 64
CTX_LEN = 256
CONV_W = 256
FOUR_GROUPS = 4
FOUR_GROUP_DIM = 64
FOUR_W = FOUR_GROUPS * FOUR_GROUP_DIM
N_HEADS = 8
N_KV = 2
GROUP = N_HEADS // N_KV
HEAD_DIM = 64
ATT_W = N_HEADS * HEAD_DIM
KV_W = N_KV * HEAD_DIM
Q_START = 3 * CONV_W + FOUR_W
K_START = Q_START + ATT_W
V_START = K_START + KV_W
PROJ_W = V_START + KV_W
AXIS_DIM = HEAD_DIM // 2
ROPE_THETA = 10000.0
D_FF = 4 * D_MODEL
N_MOD = 6
EPS = 1e-6

N_LAT = BATCH * SEQ
N_CTX = BATCH * CTX_LEN
N_TOK = N_LAT + N_CTX
MOD_ROWS = 8
CTX_MOD_ROW = BATCH

LANES = 128
BF16_SUBLANES = 16
MXU_W = 256
VMEM_LIMIT_BYTES = 56 * 1024 * 1024

TM = 1024
SUB = 256
ADA_TN = 1024
TQ = 256
KT = 256
Q_PER_STEP = 2
SCORE_LEAD = 6
LOG2_E = math.log2(math.e)
MIX_W_ROWS = CONV_W + FOUR_W + ATT_W
DFT_R = 64
F2_GROUP = 8
HALO = BF16_SUBLANES
Z_PITCH = DFT_R + 8
Z_ROWS = N_TOK // DFT_R * Z_PITCH

assert SEQ == DFT_R * DFT_R and SEQ % TM == 0 and N_CTX % TM == 0
assert CTX_LEN == TQ == KT == SUB and SEQ % (TQ * Q_PER_STEP) == 0 and TM % SUB == 0


def _params(n_axes):
    return pltpu.CompilerParams(dimension_semantics=("arbitrary",) * n_axes,
                                vmem_limit_bytes=VMEM_LIMIT_BYTES)


def _full(shape):
    return pl.BlockSpec(shape, lambda *_: (0,) * len(shape))


def _layer(l, shape):
    return pl.BlockSpec((None,) + shape, lambda *_: (l,) + (0,) * len(shape))


def _mod_spec(l, tm):
    def index(i):
        return (l, jnp.where(i < N_LAT // tm, i // (SEQ // tm), CTX_MOD_ROW), 0, 0)
    return pl.BlockSpec((None, None, N_MOD, D_MODEL), index)


def _ada_kernel(c_ref, w_ref, b_ref, o_ref):
    c = c_ref[...]
    s = c * jax.nn.sigmoid(c)
    split = lambda a: (a.astype(BF16), (a - a.astype(BF16).astype(F32)).astype(BF16))
    s_hi, s_lo = split(s)
    w_hi, w_lo = split(w_ref[...])
    head = jnp.dot(jnp.concatenate([s_hi, s_lo], axis=0), w_hi, preferred_element_type=F32)
    tail = jnp.dot(s_hi, w_lo, preferred_element_type=F32)
    o_ref[...] = head[:MOD_ROWS] + head[MOD_ROWS:] + tail + b_ref[...]


def _ada(c8, w_ada, b_ada):
    tn = ADA_TN
    n = N_MOD * D_MODEL
    return pl.pallas_call(
        _ada_kernel,
        grid=(DEPTH, n // tn),
        in_specs=[pl.BlockSpec((MOD_ROWS, D_MODEL), lambda l, j: (0, 0)),
                  pl.BlockSpec((None, D_MODEL, tn), lambda l, j: (l, 0, j)),
                  pl.BlockSpec((None, 1, tn), lambda l, j: (l, 0, j))],
        out_specs=pl.BlockSpec((None, MOD_ROWS, tn), lambda l, j: (l, 0, j)),
        out_shape=jax.ShapeDtypeStruct((DEPTH, MOD_ROWS, n), F32),
        compiler_params=_params(2),
        name="ada",
    )(c8, w_ada, b_ada.reshape(DEPTH, 1, n))


def _rms(x, g):
    return x * lax.rsqrt(jnp.mean(x * x, axis=-1, keepdims=True) + EPS) * g


def _head_norm_rope(t, ones, gain, cos, sin_lo, sin_hi):
    w = t.shape[-1]
    blk = ones.shape[0]
    tt = (t * t).astype(BF16)
    ms = jnp.concatenate([jnp.dot(tt[:, c:c + blk], ones, preferred_element_type=F32)
                          for c in range(0, w, blk)], axis=1)
    tn = t * lax.rsqrt(ms + EPS) * gain
    up = pltpu.roll(tn, w - AXIS_DIM // 2, 1)
    dn = pltpu.roll(tn, AXIS_DIM // 2, 1)
    return tn * cos + up * sin_lo + dn * sin_hi


def _token_rows(x_refs):
    if len(x_refs) == 1:
        return lambda rows: x_refs[0][rows, :]
    lat_ref, ctx_ref = x_refs
    is_lat = pl.program_id(0) < N_LAT // TM
    return lambda rows: jnp.where(is_lat, lat_ref[rows, :], ctx_ref[rows, :])


def _token_specs(split):
    row = lambda i: (i, 0)
    if not split:
        return [pl.BlockSpec((TM, D_MODEL), row)]
    n_lat_tiles = N_LAT // TM
    return [pl.BlockSpec((TM, D_MODEL), lambda i: (jnp.minimum(i, n_lat_tiles - 1), 0)),
            pl.BlockSpec((TM, D_MODEL), lambda i: (jnp.maximum(i - n_lat_tiles, 0), 0))]


def _inproj_kernel(*refs, n_x):
    (mod_ref, g_ref, w_ref, mch_ref, ones_ref, qg_ref, kg_ref, cos_ref, slo_ref, shi_ref,
     ubc_ref, zre_ref, zim_ref, qt_ref, k_ref, vt_ref) = refs[n_x:]
    read_x = _token_rows(refs[:n_x])
    rep = ATT_W // LANES
    n_sub = TM // SUB

    def project(sb):
        rows = slice(sb * SUB, (sb + 1) * SUB)
        h = _rms(read_x(rows), g_ref[...]) * (1.0 + mod_ref[1:2, :]) + mod_ref[0:1, :]
        return jnp.dot(h.astype(BF16), w_ref[...], preferred_element_type=F32)

    def finish(sb, p):
        rows = slice(sb * SUB, (sb + 1) * SUB)
        ubc_ref[rows, :] = p[:, :3 * CONV_W].astype(BF16)
        z = jnp.dot(p[:, 3 * CONV_W:Q_START].astype(BF16), mch_ref[...], preferred_element_type=F32)
        for rr in range(SUB // DFT_R):
            src = slice(rr * DFT_R, (rr + 1) * DFT_R)
            dst = pl.ds((sb * (SUB // DFT_R) + rr) * Z_PITCH, DFT_R)
            zre_ref[dst, :] = z[src, :FOUR_W]
            zim_ref[dst, :] = z[src, FOUR_W:]
        cos, slo, shi = cos_ref[rows, :], slo_ref[rows, :], shi_ref[rows, :]
        q = _head_norm_rope(p[:, Q_START:K_START], ones_ref[:MXU_W, :MXU_W], qg_ref[...],
                            jnp.tile(cos, (1, rep)), jnp.tile(slo, (1, rep)), jnp.tile(shi, (1, rep)))
        qt_ref[:, rows] = (q * (HEAD_DIM ** -0.5 * LOG2_E)).T.astype(BF16)
        k = _head_norm_rope(p[:, K_START:V_START], ones_ref[:KV_W, :KV_W], kg_ref[...], cos, slo, shi)
        k_ref[rows, :] = k.astype(BF16)
        vt_ref[sb] = p[:, V_START:].T.astype(BF16)

    for sb in range(n_sub):
        finish(sb, project(sb))


def _inproj(l, xs, mod, g, w, mch, ones, qg, kg, cos, slo, shi):
    n_tiles = N_TOK // TM
    n_lat_tiles = N_LAT // TM
    pos_tiles = SEQ // TM

    def pos_map(i):
        return (jnp.where(i < n_lat_tiles, i % pos_tiles, pos_tiles), 0)

    row = lambda i: (i, 0)
    pos_spec = pl.BlockSpec((TM, LANES), pos_map)
    return pl.pallas_call(
        functools.partial(_inproj_kernel, n_x=len(xs)),
        grid=(n_tiles,),
        in_specs=_token_specs(len(xs) == 2) + [
                  _mod_spec(l, TM),
                  _layer(l, (1, D_MODEL)),
                  _full((D_MODEL, PROJ_W)),
                  _full((FOUR_W, 2 * FOUR_W)),
                  _full((ATT_W, ATT_W)),
                  _layer(l, (1, ATT_W)),
                  _layer(l, (1, KV_W)),
                  pos_spec, pos_spec, pos_spec],
        out_specs=[pl.BlockSpec((TM, 3 * CONV_W), row),
                   pl.BlockSpec((TM // DFT_R * Z_PITCH, FOUR_W), row),
                   pl.BlockSpec((TM // DFT_R * Z_PITCH, FOUR_W), row),
                   pl.BlockSpec((ATT_W, TM), lambda i: (0, i)),
                   pl.BlockSpec((TM, KV_W), row),
                   pl.BlockSpec((TM // KT, KV_W, KT), lambda i: (i, 0, 0))],
        out_shape=[jax.ShapeDtypeStruct((N_TOK, 3 * CONV_W), BF16),
                   jax.ShapeDtypeStruct((Z_ROWS, FOUR_W), F32),
                   jax.ShapeDtypeStruct((Z_ROWS, FOUR_W), F32),
                   jax.ShapeDtypeStruct((ATT_W, N_TOK), BF16),
                   jax.ShapeDtypeStruct((N_TOK, KV_W), BF16),
                   jax.ShapeDtypeStruct((N_TOK // KT, KV_W, KT), BF16)],
        compiler_params=_params(1),
        name="in_proj",
    )(*xs, mod, g, w, mch, ones, qg, kg, cos, slo, shi)


def _attn_body(qt_ref, kc_ref, vtc_ref, kl_ref, vtl_ref, o_ref, ot_ref, *, n_lat_tiles):
    sub = 8
    ones = jnp.ones((BF16_SUBLANES, KT), BF16)

    n_tiles = 1 + n_lat_tiles
    n_q = qt_ref.shape[1] // TQ
    work = [(qi, g, j) for qi in range(n_q) for g in range(N_HEADS) for j in range(n_tiles)]

    def scores(qi, g, j):
        qt = qt_ref[g * HEAD_DIM:(g + 1) * HEAD_DIM, qi * TQ:(qi + 1) * TQ]
        zero = jnp.zeros_like(qt)
        w = jnp.concatenate([qt, zero] if g // GROUP == 0 else [zero, qt], axis=0)
        k_tile = kc_ref[...] if j == 0 else kl_ref[(j - 1) * KT:j * KT, :]
        return jnp.dot(k_tile, w, preferred_element_type=F32)

    def accumulate(g, j, s, m, acc):
        hv = g // GROUP
        v_rows = slice(hv * HEAD_DIM, (hv + 1) * HEAD_DIM)
        vt_tile = vtc_ref[0, v_rows, :] if j == 0 else vtl_ref[j - 1, v_rows, :]
        tile_max = jnp.max(jnp.max(s.reshape(KT // sub, sub, TQ), axis=0), axis=0, keepdims=True)
        m_new = tile_max if j == 0 else jnp.maximum(m, tile_max)
        p = jnp.exp2(s - m_new).astype(BF16)
        lhs = jnp.concatenate([vt_tile, ones], axis=0)
        pv = jnp.dot(lhs, p, preferred_element_type=F32)
        if j > 0:
            pv = pv + acc * jnp.exp2(m - m_new)
        return m_new, pv

    pending = {}
    m = acc = None
    for step in range(len(work) + SCORE_LEAD):
        if step < len(work):
            pending[step] = scores(*work[step])
        if step >= SCORE_LEAD:
            qi, g, j = work[step - SCORE_LEAD]
            m, acc = accumulate(g, j, pending.pop(step - SCORE_LEAD), m, acc)
            if j == n_tiles - 1:
                ot_ref[g * HEAD_DIM:(g + 1) * HEAD_DIM, :] = acc[:HEAD_DIM] / acc[HEAD_DIM:HEAD_DIM + 1]
                if g == N_HEADS - 1:
                    o_ref[qi * TQ:(qi + 1) * TQ, :] = ot_ref[...].T.astype(o_ref.dtype)


def _attn_lat_kernel(*refs, n_cast):
    qt_ref, kc_ref, vtc_ref, kl_ref, vtl_ref = refs[:5]
    w_refs = refs[5:5 + n_cast]
    o_ref = refs[5 + n_cast]
    w_out_refs = refs[6 + n_cast:6 + 2 * n_cast]
    ot_ref = refs[6 + 2 * n_cast]
    _attn_body(qt_ref, kc_ref, vtc_ref, kl_ref, vtl_ref, o_ref, ot_ref, n_lat_tiles=SEQ // KT)
    for w_ref, w_out_ref in zip(w_refs, w_out_refs):
        w_out_ref[...] = w_ref[...].astype(BF16)


def _attn_ctx_kernel(qt_ref, kc_ref, vtc_ref, prev_ref, o_ref, ot_ref):
    del prev_ref
    _attn_body(qt_ref, kc_ref, vtc_ref, None, None, o_ref, ot_ref, n_lat_tiles=0)


def _attn_scratch():
    return [pltpu.VMEM((ATT_W, TQ), F32)]


def _attn_lat(qt, k, vt, casts):
    tq = TQ * Q_PER_STEP
    nq = SEQ // tq
    n_steps = BATCH * nq
    ctx0 = N_LAT // CTX_LEN
    cast_in, cast_out, cast_shape = [], [], []
    for w, layer in casts:
        n_rows, n_cols = w.shape[1:]
        blk = (n_rows // n_steps, n_cols)
        cast_in.append(pl.BlockSpec((None,) + blk, lambda b, j, layer=layer: (layer, b * nq + j, 0)))
        cast_out.append(pl.BlockSpec(blk, lambda b, j: (b * nq + j, 0)))
        cast_shape.append(jax.ShapeDtypeStruct((n_rows, n_cols), BF16))
    return pl.pallas_call(
        functools.partial(_attn_lat_kernel, n_cast=len(casts)),
        grid=(BATCH, nq),
        in_specs=[pl.BlockSpec((ATT_W, tq), lambda b, j: (0, b * nq + j)),
                  pl.BlockSpec((CTX_LEN, KV_W), lambda b, j: (ctx0 + b, 0)),
                  pl.BlockSpec((CTX_LEN // KT, KV_W, KT), lambda b, j: (ctx0 + b, 0, 0)),
                  pl.BlockSpec((SEQ, KV_W), lambda b, j: (b, 0)),
                  pl.BlockSpec((SEQ // KT, KV_W, KT), lambda b, j: (b, 0, 0))] + cast_in,
        out_specs=[pl.BlockSpec((tq, ATT_W), lambda b, j: (b * nq + j, 0))] + cast_out,
        out_shape=[jax.ShapeDtypeStruct((N_TOK, ATT_W), BF16)] + cast_shape,
        scratch_shapes=_attn_scratch(),
        compiler_params=_params(2),
        name="attn_latent",
    )(qt, k, vt, k, vt, *[w for w, _ in casts])


def _attn_ctx(qt, k, vt, att):
    ctx0 = N_LAT // CTX_LEN
    return pl.pallas_call(
        _attn_ctx_kernel,
        grid=(BATCH,),
        in_specs=[pl.BlockSpec((ATT_W, TQ), lambda b: (0, ctx0 + b)),
                  pl.BlockSpec((CTX_LEN, KV_W), lambda b: (ctx0 + b, 0)),
                  pl.BlockSpec((CTX_LEN // KT, KV_W, KT), lambda b: (ctx0 + b, 0, 0)),
                  pl.BlockSpec(memory_space=pl.ANY)],
        out_specs=pl.BlockSpec((TQ, ATT_W), lambda b: (ctx0 + b, 0)),
        out_shape=jax.ShapeDtypeStruct((N_TOK, ATT_W), BF16),
        input_output_aliases={3: 0},
        scratch_shapes=_attn_scratch(),
        compiler_params=_params(1),
        name="attn_context",
    )(qt, k, vt, att)


def _four_lat_kernel(zre0_ref, zre1_ref, zim0_ref, zim1_ref, cs_ref, tc_ref, ts_ref, o_ref,
                     are0_ref, are1_ref, aim0_ref, aim1_ref):
    rot = cs_ref[...]
    both = lambda lo_ref, hi_ref, idx: jnp.concatenate([lo_ref[idx, :], hi_ref[idx, :]], axis=1)
    for c in range(DFT_R):
        strided = pl.ds(c, DFT_R, stride=Z_PITCH)
        z = jnp.concatenate([both(zre0_ref, zre1_ref, strided), both(zim0_ref, zim1_ref, strided)], axis=0)
        a = jnp.dot(rot, z.astype(BF16), preferred_element_type=F32)
        are, aim = a[:DFT_R], a[DFT_R:]
        are0_ref[strided, :] = are[:, :LANES]
        are1_ref[strided, :] = are[:, LANES:]
        aim0_ref[strided, :] = aim[:, :LANES]
        aim1_ref[strided, :] = aim[:, LANES:]

    def group(lo_ref, hi_ref, m):
        parts = [both(lo_ref, hi_ref, pl.ds((m * F2_GROUP + g) * Z_PITCH, DFT_R)) for g in range(F2_GROUP)]
        return jnp.concatenate(parts, axis=0).astype(BF16)

    for m in range(DFT_R // F2_GROUP):
        y = (jnp.dot(tc_ref[m], group(are0_ref, are1_ref, m), preferred_element_type=F32)
             + jnp.dot(ts_ref[m], group(aim0_ref, aim1_ref, m), preferred_element_type=F32))
        o_ref[:, m, :, :] = y.reshape(DFT_R, F2_GROUP, FOUR_W)


def _four_lat(zre, zim, cs64, tc, ts):
    n_groups = DFT_R // F2_GROUP
    rows = F2_GROUP * DFT_R
    seq_rows = DFT_R * Z_PITCH
    lo = pl.BlockSpec((seq_rows, LANES), lambda b: (b, 0))
    hi = pl.BlockSpec((seq_rows, LANES), lambda b: (b, 1))
    t_spec = pl.BlockSpec((n_groups, rows, rows), lambda b: (0, 0, 0), pipeline_mode=pl.Buffered(1))
    yf = pl.pallas_call(
        _four_lat_kernel,
        grid=(BATCH,),
        in_specs=[lo, hi, lo, hi, _full((2 * DFT_R, 2 * DFT_R)), t_spec, t_spec],
        out_specs=pl.BlockSpec((DFT_R, n_groups, F2_GROUP, FOUR_W), lambda b: (b, 0, 0, 0)),
        out_shape=jax.ShapeDtypeStruct((N_TOK // DFT_R, n_groups, F2_GROUP, FOUR_W), F32),
        scratch_shapes=[pltpu.VMEM((seq_rows, LANES), F32)] * 4,
        compiler_params=_params(1),
        name="fourier_latent",
    )(zre, zre, zim, zim, cs64, tc, ts)
    return yf.reshape(N_TOK, FOUR_W)


def _four_ctx_kernel(zre_ref, zim_ref, c_ref, s_ref, prev_ref, o_ref):
    del prev_ref
    rows = lambda ref: jnp.concatenate(
        [ref[pl.ds(r * Z_PITCH, DFT_R), :] for r in range(CTX_LEN // DFT_R)], axis=0).astype(BF16)
    o_ref[...] = (jnp.dot(c_ref[...], rows(zre_ref), preferred_element_type=F32)
                  + jnp.dot(s_ref[...], rows(zim_ref), preferred_element_type=F32))


def _four_ctx(zre, zim, c256, s256, yf):
    ctx0 = N_LAT // CTX_LEN
    z_spec = pl.BlockSpec((CTX_LEN // DFT_R * Z_PITCH, FOUR_W), lambda b: (ctx0 + b, 0))
    blk = pl.BlockSpec((CTX_LEN, FOUR_W), lambda b: (ctx0 + b, 0))
    return pl.pallas_call(
        _four_ctx_kernel,
        grid=(BATCH,),
        in_specs=[z_spec, z_spec, _full((CTX_LEN, CTX_LEN)), _full((CTX_LEN, CTX_LEN)),
                  pl.BlockSpec(memory_space=pl.ANY)],
        out_specs=blk,
        out_shape=jax.ShapeDtypeStruct((N_TOK, FOUR_W), F32),
        input_output_aliases={4: 0},
        compiler_params=_params(1),
        name="fourier_context",
    )(zre, zim, c256, s256, yf)


def _outproj_kernel(*refs, n_x):
    ubc_ref, hp_ref, hn_ref, yf_ref, att_ref = refs[:5]
    mod_ref, g_ref, cw_ref, wo_ref, o_ref = refs[5 + n_x:]
    read_x = _token_rows(refs[5:5 + n_x])
    i = pl.program_id(0)
    ubc = ubc_ref[...].astype(F32)
    u, bg, cg = ubc[:, :CONV_W], ubc[:, CONV_W:2 * CONV_W], ubc[:, 2 * CONV_W:]
    z = cg * u
    hp = hp_ref[HALO - 1:HALO, :].astype(F32)
    hn = hn_ref[0:1, :].astype(F32)
    z_before = hp[:, 2 * CONV_W:] * hp[:, :CONV_W]
    z_after = hn[:, 2 * CONV_W:] * hn[:, :CONV_W]
    row = lax.broadcasted_iota(jnp.int32, (SUB, 1), 0)
    w_mix = jnp.concatenate([wo_ref[CONV_W + FOUR_W:, :], wo_ref[CONV_W:CONV_W + FOUR_W, :],
                             wo_ref[:CONV_W, :]], axis=0)
    gate_gain = mod_ref[2:3, :] * g_ref[...]
    for sb in range(TM // SUB):
        rows = slice(sb * SUB, (sb + 1) * SUB)
        tok = i * TM + sb * SUB
        seg = jnp.where(tok < N_LAT, SEQ, CTX_LEN)
        starts = jnp.bitwise_and(tok, seg - 1) == 0
        ends = jnp.bitwise_and(tok + SUB, seg - 1) == 0
        above = z_before if sb == 0 else z[sb * SUB - 1:sb * SUB]
        below = z_after if (sb + 1) * SUB == TM else z[(sb + 1) * SUB:(sb + 1) * SUB + 1]
        above = jnp.where(starts, 0.0, above)
        below = jnp.where(ends, 0.0, below)
        zs = z[rows]
        z_prev = jnp.where(row == 0, above, pltpu.roll(zs, 1, 0))
        z_next = jnp.where(row == SUB - 1, below, pltpu.roll(zs, SUB - 1, 0))
        conv = bg[rows] * (z_prev * cw_ref[0:1, :] + zs * cw_ref[1:2, :] + z_next * cw_ref[2:3, :])
        mix = jnp.concatenate([att_ref[rows, :], yf_ref[rows, :].astype(BF16), conv.astype(BF16)], axis=1)
        y = jnp.dot(mix, w_mix, preferred_element_type=F32)
        scale = lax.rsqrt(jnp.mean(y * y, axis=-1, keepdims=True) + EPS)
        o_ref[rows, :] = read_x(rows) + y * scale * gate_gain


def _outproj(l, n_rows, ubc, yf, att, xs, mod, g, cw, wo):
    n_tiles = n_rows // TM
    per = TM // HALO
    last_halo = n_rows // HALO - 1
    row = lambda i: (i, 0)
    return pl.pallas_call(
        functools.partial(_outproj_kernel, n_x=len(xs)),
        grid=(n_tiles,),
        in_specs=[pl.BlockSpec((TM, 3 * CONV_W), row),
                  pl.BlockSpec((HALO, 3 * CONV_W), lambda i: (jnp.maximum(i * per - 1, 0), 0)),
                  pl.BlockSpec((HALO, 3 * CONV_W), lambda i: (jnp.minimum((i + 1) * per, last_halo), 0)),
                  pl.BlockSpec((TM, FOUR_W), row),
                  pl.BlockSpec((TM, ATT_W), row)] + _token_specs(len(xs) == 2) + [
                  _mod_spec(l, TM),
                  _layer(l, (1, D_MODEL)),
                  _layer(l, (3, CONV_W)),
                  _full((MIX_W_ROWS, D_MODEL))],
        out_specs=pl.BlockSpec((TM, D_MODEL), row),
        out_shape=jax.ShapeDtypeStruct((n_rows, D_MODEL), F32),
        compiler_params=_params(1),
        name="out_proj",
    )(ubc, ubc, ubc, yf, att, *xs, mod, g, cw, wo)


FF_CHUNK = 1024
FF_LEAD = 1


def _ffn_kernel(x_ref, mod_ref, gpre_ref, gpost_ref, wu_ref, wd_ref, o_ref):
    n_chunks = D_FF // FF_CHUNK
    work = [(sb, c) for sb in range(TM // SUB) for c in range(n_chunks)]

    normed = {}

    def up(sb, c):
        if sb not in normed:
            x = x_ref[sb * SUB:(sb + 1) * SUB, :]
            normed[sb] = (_rms(x, gpre_ref[...]) * (1.0 + mod_ref[4:5, :]) + mod_ref[3:4, :]).astype(BF16)
        return jnp.dot(normed[sb], wu_ref[:, c * FF_CHUNK:(c + 1) * FF_CHUNK], preferred_element_type=F32)

    def down(c, a):
        a = jnp.maximum(a, 0.0)
        return jnp.dot((a * a).astype(BF16), wd_ref[c * FF_CHUNK:(c + 1) * FF_CHUNK, :],
                       preferred_element_type=F32)

    pending = {}
    acc = None
    for step in range(len(work) + FF_LEAD):
        if step < len(work):
            pending[step] = up(*work[step])
        if step >= FF_LEAD:
            sb, c = work[step - FF_LEAD]
            d = down(c, pending.pop(step - FF_LEAD))
            acc = d if c == 0 else acc + d
            if c == n_chunks - 1:
                rows = slice(sb * SUB, (sb + 1) * SUB)
                o_ref[rows, :] = x_ref[rows, :] + mod_ref[5:6, :] * _rms(acc, gpost_ref[...])


def _ffn(l, n_rows, x, mod, gpre, gpost, wu, wd):
    row = lambda i: (i, 0)
    return pl.pallas_call(
        _ffn_kernel,
        grid=(n_rows // TM,),
        in_specs=[pl.BlockSpec((TM, D_MODEL), row),
                  _mod_spec(l, TM),
                  _layer(l, (1, D_MODEL)),
                  _layer(l, (1, D_MODEL)),
                  pl.BlockSpec((D_MODEL, D_FF), lambda i: (0, 0), pipeline_mode=pl.Buffered(1)),
                  pl.BlockSpec((D_FF, D_MODEL), lambda i: (0, 0), pipeline_mode=pl.Buffered(1))],
        out_specs=pl.BlockSpec((TM, D_MODEL), row),
        out_shape=jax.ShapeDtypeStruct((n_rows, D_MODEL), F32),
        compiler_params=_params(1),
        name="ffn",
    )(x, mod, gpre, gpost, wu, wd)


def _rope_tables():
    t = jnp.arange(SEQ)
    rowp = (t // GRID_W).astype(F32)
    colp = (t % GRID_W).astype(F32)
    inv = 1.0 / (ROPE_THETA ** (jnp.arange(0, AXIS_DIM, 2, dtype=F32) / AXIS_DIM))
    ang = jnp.stack([rowp[:, None] * inv, colp[:, None] * inv], axis=1)
    ang = jnp.broadcast_to(ang[:, :, None, :], (SEQ, 2, 2, AXIS_DIM // 2)).reshape(SEQ, HEAD_DIM)
    cos, sin = jnp.cos(ang), jnp.sin(ang)
    first_half = (jnp.arange(HEAD_DIM) % AXIS_DIM) < AXIS_DIM // 2
    sin_lo = jnp.where(first_half, -sin, 0.0)
    sin_hi = jnp.where(first_half, 0.0, sin)
    ident = lambda v: jnp.full((TM, HEAD_DIM), v, F32)
    wide = lambda a, v: jnp.tile(jnp.concatenate([a, ident(v)], axis=0), (1, LANES // HEAD_DIM))
    return wide(cos, 1.0), wide(sin_lo, 0.0), wide(sin_hi, 0.0)


def _dft_tables():
    two_pi = 2.0 * np.pi
    n = np.arange(FOUR_GROUP_DIM)
    ang = two_pi * np.outer(n, n) / FOUR_GROUP_DIM
    eye = np.eye(FOUR_GROUPS)
    scale = FOUR_GROUP_DIM ** -0.5
    mch = np.concatenate([np.kron(eye, np.cos(ang)), -np.kron(eye, np.sin(ang))], axis=1) * scale
    r = np.arange(DFT_R)
    ang_r = two_pi * np.outer(r, r) / DFT_R
    cs64 = np.block([[np.cos(ang_r), np.sin(ang_r)], [-np.sin(ang_r), np.cos(ang_r)]])
    n_groups = DFT_R // F2_GROUP
    m = np.arange(n_groups)[:, None, None, None]
    k2 = np.arange(DFT_R)[None, :, None, None]
    g = np.arange(F2_GROUP)[None, None, :, None]
    c = np.arange(DFT_R)[None, None, None, :]
    k = F2_GROUP * m + g + DFT_R * k2
    ang2 = two_pi * ((k * c) % SEQ) / SEQ
    sel = np.eye(F2_GROUP)[None, None, :, :, None]
    norm2 = SEQ ** -0.5
    tc = (np.cos(ang2)[:, :, :, None, :] * sel * norm2).reshape(n_groups, DFT_R * F2_GROUP, F2_GROUP * DFT_R)
    ts = (np.sin(ang2)[:, :, :, None, :] * sel * norm2).reshape(n_groups, DFT_R * F2_GROUP, F2_GROUP * DFT_R)
    t = np.arange(CTX_LEN)
    ang_c = two_pi * (np.outer(t, t) % CTX_LEN) / CTX_LEN
    normc = CTX_LEN ** -0.5
    c256, s256 = np.cos(ang_c) * normc, np.sin(ang_c) * normc
    as_bf16 = lambda a: jnp.asarray(a, dtype=F32).astype(BF16)
    return tuple(as_bf16(a) for a in (mch, cs64, tc, ts, c256, s256))


def _head_mean_matrix():
    blocks = np.kron(np.eye(N_HEADS), np.full((HEAD_DIM, HEAD_DIM), 1.0 / HEAD_DIM))
    return jnp.asarray(blocks, dtype=F32).astype(BF16)


def kernel(x, c, ctx, c_ctx, w_ada, b_ada, w_in, conv_w, q_gain, k_gain, w_out,
           g_pre_mix, g_post_mix, g_pre_ffn, g_post_ffn, w_ffn_up, w_ffn_down):
    cos, sin_lo, sin_hi = _rope_tables()
    mch, cs64, tc, ts, c256, s256 = _dft_tables()
    ones = _head_mean_matrix()

    c8 = jnp.concatenate([c, c_ctx[None, :], jnp.zeros((MOD_ROWS - BATCH - 1, D_MODEL), F32)], axis=0)
    mod = _ada(c8, w_ada, b_ada).reshape(DEPTH, MOD_ROWS, N_MOD, D_MODEL)

    xs = (x.reshape(N_LAT, D_MODEL), ctx.reshape(N_CTX, D_MODEL))
    w_in_b = w_in[0].astype(BF16)
    rows = lambda a: a.reshape(DEPTH, 1, -1)
    qg, kg = rows(jnp.tile(q_gain, (1, N_HEADS))), rows(jnp.tile(k_gain, (1, N_KV)))
    g_pre_mix, g_post_mix, g_pre_ffn, g_post_ffn = map(rows, (g_pre_mix, g_post_mix, g_pre_ffn, g_post_ffn))

    for l in range(DEPTH):
        last = l == DEPTH - 1
        ubc, zre, zim, qt, k, vt = _inproj(l, xs, mod, g_pre_mix, w_in_b, mch, ones,
                                           qg, kg, cos, sin_lo, sin_hi)
        casts = [(w_out, l), (w_ffn_up, l), (w_ffn_down, l)] + ([] if last else [(w_in, l + 1)])
        att, w_out_b, w_up_b, w_down_b, *w_in_next = _attn_lat(qt, k, vt, casts)
        if not last:
            w_in_b = w_in_next[0]
        yf = _four_lat(zre, zim, cs64, tc, ts)
        if last:
            n_rows = N_LAT
        else:
            n_rows = N_TOK
            att = _attn_ctx(qt, k, vt, att)
            yf = _four_ctx(zre, zim, c256, s256, yf)
        x_mid = _outproj(l, n_rows, ubc, yf, att, xs, mod, g_post_mix, conv_w, w_out_b)
        xs = (_ffn(l, n_rows, x_mid, mod, g_pre_ffn, g_post_ffn, w_up_b, w_down_b),)
    return xs[0].reshape(BATCH, SEQ, D_MODEL)
```

```python
import functools
import math

import jax
import jax.numpy as jnp
import numpy as np
from jax import lax
from jax.experimental import pallas as pl
from jax.experimental.pallas import tpu as pltpu

F32 = jnp.float32
BF16 = jnp.bfloat16

D_MODEL = 1024
BATCH = 4
SEQ = 4096
DEPTH = 4
GRID_W = 64
CTX_LEN = 256
CONV_W = 256
FOUR_GROUPS = 4
FOUR_GROUP_DIM = 64
FOUR_W = FOUR_GROUPS * FOUR_GROUP_DIM
N_HEADS = 8
N_KV = 2
GROUP = N_HEADS // N_KV
HEAD_DIM = 64
ATT_W = N_HEADS * HEAD_DIM
KV_W = N_KV * HEAD_DIM
Q_START = 3 * CONV_W + FOUR_W
K_START = Q_START + ATT_W
V_START = K_START + KV_W
PROJ_W = V_START + KV_W
AXIS_DIM = HEAD_DIM // 2
ROPE_THETA = 10000.0
D_FF = 4 * D_MODEL
N_MOD = 6
EPS = 1e-6

N_LAT = BATCH * SEQ
N_CTX = BATCH * CTX_LEN
N_TOK = N_LAT + N_CTX
MOD_ROWS = 8
CTX_MOD_ROW = BATCH

LANES = 128
BF16_SUBLANES = 16
MXU_W = 256
VMEM_LIMIT_BYTES = 56 * 1024 * 1024

TM = 1024
SUB = 256
ADA_TN = 1024
TQ = 256
KT = 256
Q_PER_STEP = 2
SCORE_LEAD = 6
LOG2_E = math.log2(math.e)
MIX_W_ROWS = CONV_W + FOUR_W + ATT_W
DFT_R = 64
F2_GROUP = 8
HALO = BF16_SUBLANES
Z_PITCH = DFT_R + 8
Z_ROWS = N_TOK // DFT_R * Z_PITCH

assert SEQ == DFT_R * DFT_R and SEQ % TM == 0 and N_CTX % TM == 0
assert CTX_LEN == TQ == KT == SUB and SEQ % (TQ * Q_PER_STEP) == 0 and TM % SUB == 0


def _params(n_axes):
    return pltpu.CompilerParams(dimension_semantics=("arbitrary",) * n_axes,
                                vmem_limit_bytes=VMEM_LIMIT_BYTES)


def _full(shape):
    return pl.BlockSpec(shape, lambda *_: (0,) * len(shape))


def _layer(l, shape):
    return pl.BlockSpec((None,) + shape, lambda *_: (l,) + (0,) * len(shape))


def _mod_spec(l, tm):
    def index(i):
        return (l, jnp.where(i < N_LAT // tm, i // (SEQ // tm), CTX_MOD_ROW), 0, 0)
    return pl.BlockSpec((None, None, N_MOD, D_MODEL), index)


def _ada_kernel(c_ref, w_ref, b_ref, o_ref):
    c = c_ref[...]
    s = c * jax.nn.sigmoid(c)
    split = lambda a: (a.astype(BF16), (a - a.astype(BF16).astype(F32)).astype(BF16))
    s_hi, s_lo = split(s)
    w_hi, w_lo = split(w_ref[...])
    head = jnp.dot(jnp.concatenate([s_hi, s_lo], axis=0), w_hi, preferred_element_type=F32)
    tail = jnp.dot(s_hi, w_lo, preferred_element_type=F32)
    o_ref[...] = head[:MOD_ROWS] + head[MOD_ROWS:] + tail + b_ref[...]


def _ada(c8, w_ada, b_ada):
    tn = ADA_TN
    n = N_MOD * D_MODEL
    return pl.pallas_call(
        _ada_kernel,
        grid=(DEPTH, n // tn),
        in_specs=[pl.BlockSpec((MOD_ROWS, D_MODEL), lambda l, j: (0, 0)),
                  pl.BlockSpec((None, D_MODEL, tn), lambda l, j: (l, 0, j)),
                  pl.BlockSpec((None, 1, tn), lambda l, j: (l, 0, j))],
        out_specs=pl.BlockSpec((None, MOD_ROWS, tn), lambda l, j: (l, 0, j)),
        out_shape=jax.ShapeDtypeStruct((DEPTH, MOD_ROWS, n), F32),
        compiler_params=_params(2),
        name="ada",
    )(c8, w_ada, b_ada.reshape(DEPTH, 1, n))


def _rms(x, g):
    return x * lax.rsqrt(jnp.mean(x * x, axis=-1, keepdims=True) + EPS) * g


def _head_norm_rope(t, ones, gain, cos, sin_lo, sin_hi):
    w = t.shape[-1]
    blk = ones.shape[0]
    tt = (t * t).astype(BF16)
    ms = jnp.concatenate([jnp.dot(tt[:, c:c + blk], ones, preferred_element_type=F32)
                          for c in range(0, w, blk)], axis=1)
    tn = t * lax.rsqrt(ms + EPS) * gain
    up = pltpu.roll(tn, w - AXIS_DIM // 2, 1)
    dn = pltpu.roll(tn, AXIS_DIM // 2, 1)
    return tn * cos + up * sin_lo + dn * sin_hi


def _token_rows(x_refs):
    if len(x_refs) == 1:
        return lambda rows: x_refs[0][rows, :]
    lat_ref, ctx_ref = x_refs
    is_lat = pl.program_id(0) < N_LAT // TM
    return lambda rows: jnp.where(is_lat, lat_ref[rows, :], ctx_ref[rows, :])


def _token_specs(split):
    row = lambda i: (i, 0)
    if not split:
        return [pl.BlockSpec((TM, D_MODEL), row)]
    n_lat_tiles = N_LAT // TM
    return [pl.BlockSpec((TM, D_MODEL), lambda i: (jnp.minimum(i, n_lat_tiles - 1), 0)),
            pl.BlockSpec((TM, D_MODEL), lambda i: (jnp.maximum(i - n_lat_tiles, 0), 0))]


def _inproj_kernel(*refs, n_x):
    (mod_ref, g_ref, w_ref, mch_ref, ones_ref, qg_ref, kg_ref, cos_ref, slo_ref, shi_ref,
     ubc_ref, zre_ref, zim_ref, qt_ref, k_ref, vt_ref) = refs[n_x:]
    read_x = _token_rows(refs[:n_x])
    rep = ATT_W // LANES
    n_sub = TM // SUB

    def project(sb):
        rows = slice(sb * SUB, (sb + 1) * SUB)
        h = _rms(read_x(rows), g_ref[...] * (1.0 + mod_ref[1:2, :])) + mod_ref[0:1, :]
        return jnp.dot(h.astype(BF16), w_ref[...], preferred_element_type=F32)

    def finish(sb, p):
        rows = slice(sb * SUB, (sb + 1) * SUB)
        ubc_ref[rows, :] = p[:, :3 * CONV_W].astype(BF16)
        z = jnp.dot(p[:, 3 * CONV_W:Q_START].astype(BF16), mch_ref[...], preferred_element_type=F32)
        for rr in range(SUB // DFT_R):
            src = slice(rr * DFT_R, (rr + 1) * DFT_R)
            dst = pl.ds((sb * (SUB // DFT_R) + rr) * Z_PITCH, DFT_R)
            zre_ref[dst, :] = z[src, :FOUR_W]
            zim_ref[dst, :] = z[src, FOUR_W:]
        cos, slo, shi = cos_ref[rows, :], slo_ref[rows, :], shi_ref[rows, :]
        q = _head_norm_rope(p[:, Q_START:K_START], ones_ref[:MXU_W, :MXU_W], qg_ref[...],
                            jnp.tile(cos, (1, rep)), jnp.tile(slo, (1, rep)), jnp.tile(shi, (1, rep)))
        qt_ref[:, rows] = (q * (HEAD_DIM ** -0.5 * LOG2_E)).T.astype(BF16)
        k = _head_norm_rope(p[:, K_START:V_START], ones_ref[:KV_W, :KV_W], kg_ref[...], cos, slo, shi)
        k_ref[rows, :] = k.astype(BF16)
        vt_ref[sb] = p[:, V_START:].T.astype(BF16)

    for sb in range(n_sub):
        finish(sb, project(sb))


def _inproj(l, xs, mod, g, w, mch, ones, qg, kg, cos, slo, shi):
    n_tiles = N_TOK // TM
    n_lat_tiles = N_LAT // TM
    pos_tiles = SEQ // TM

    def pos_map(i):
        return (jnp.where(i < n_lat_tiles, i % pos_tiles, pos_tiles), 0)

    row = lambda i: (i, 0)
    pos_spec = pl.BlockSpec((TM, LANES), pos_map)
    return pl.pallas_call(
        functools.partial(_inproj_kernel, n_x=len(xs)),
        grid=(n_tiles,),
        in_specs=_token_specs(len(xs) == 2) + [
                  _mod_spec(l, TM),
                  _layer(l, (1, D_MODEL)),
                  _full((D_MODEL, PROJ_W)),
                  _full((FOUR_W, 2 * FOUR_W)),
                  _full((ATT_W, ATT_W)),
                  _layer(l, (1, ATT_W)),
                  _layer(l, (1, KV_W)),
                  pos_spec, pos_spec, pos_spec],
        out_specs=[pl.BlockSpec((TM, 3 * CONV_W), row),
                   pl.BlockSpec((TM // DFT_R * Z_PITCH, FOUR_W), row),
                   pl.BlockSpec((TM // DFT_R * Z_PITCH, FOUR_W), row),
                   pl.BlockSpec((ATT_W, TM), lambda i: (0, i)),
                   pl.BlockSpec((TM, KV_W), row),
                   pl.BlockSpec((TM // KT, KV_W, KT), lambda i: (i, 0, 0))],
        out_shape=[jax.ShapeDtypeStruct((N_TOK, 3 * CONV_W), BF16),
                   jax.ShapeDtypeStruct((Z_ROWS, FOUR_W), F32),
                   jax.ShapeDtypeStruct((Z_ROWS, FOUR_W), F32),
                   jax.ShapeDtypeStruct((ATT_W, N_TOK), BF16),
                   jax.ShapeDtypeStruct((N_TOK, KV_W), BF16),
                   jax.ShapeDtypeStruct((N_TOK // KT, KV_W, KT), BF16)],
        compiler_params=_params(1),
        name="in_proj",
    )(*xs, mod, g, w, mch, ones, qg, kg, cos, slo, shi)


def _attn_body(qt_ref, kc_ref, vtc_ref, kl_ref, vtl_ref, o_ref, ot_ref, *, n_lat_tiles):
    sub = 8
    ones = jnp.ones((BF16_SUBLANES, KT), BF16)

    n_tiles = 1 + n_lat_tiles
    n_q = qt_ref.shape[1] // TQ
    work = [(qi, g, j) for qi in range(n_q) for g in range(N_HEADS) for j in range(n_tiles)]

    def scores(qi, g, j):
        qt = qt_ref[g * HEAD_DIM:(g + 1) * HEAD_DIM, qi * TQ:(qi + 1) * TQ]
        zero = jnp.zeros_like(qt)
        w = jnp.concatenate([qt, zero] if g // GROUP == 0 else [zero, qt], axis=0)
        k_tile = kc_ref[...] if j == 0 else kl_ref[(j - 1) * KT:j * KT, :]
        return jnp.dot(k_tile, w, preferred_element_type=F32)

    def accumulate(g, j, s, m, acc):
        hv = g // GROUP
        v_rows = slice(hv * HEAD_DIM, (hv + 1) * HEAD_DIM)
        vt_tile = vtc_ref[0, v_rows, :] if j == 0 else vtl_ref[j - 1, v_rows, :]
        tile_max = jnp.max(jnp.max(s.reshape(KT // sub, sub, TQ), axis=0), axis=0, keepdims=True)
        m_new = tile_max if j == 0 else jnp.maximum(m, tile_max)
        p = jnp.exp2(s - m_new).astype(BF16)
        lhs = jnp.concatenate([vt_tile, ones], axis=0)
        pv = jnp.dot(lhs, p, preferred_element_type=F32)
        if j > 0:
            pv = pv + acc * jnp.exp2(m - m_new)
        return m_new, pv

    pending = {}
    m = acc = None
    for step in range(len(work) + SCORE_LEAD):
        if step < len(work):
            pending[step] = scores(*work[step])
        if step >= SCORE_LEAD:
            qi, g, j = work[step - SCORE_LEAD]
            m, acc = accumulate(g, j, pending.pop(step - SCORE_LEAD), m, acc)
            if j == n_tiles - 1:
                ot_ref[g * HEAD_DIM:(g + 1) * HEAD_DIM, :] = acc[:HEAD_DIM] / acc[HEAD_DIM:HEAD_DIM + 1]
                if g == N_HEADS - 1:
                    o_ref[qi * TQ:(qi + 1) * TQ, :] = ot_ref[...].T.astype(o_ref.dtype)


def _attn_lat_kernel(*refs, n_cast):
    qt_ref, kc_ref, vtc_ref, kl_ref, vtl_ref = refs[:5]
    w_refs = refs[5:5 + n_cast]
    o_ref = refs[5 + n_cast]
    w_out_refs = refs[6 + n_cast:6 + 2 * n_cast]
    ot_ref = refs[6 + 2 * n_cast]
    _attn_body(qt_ref, kc_ref, vtc_ref, kl_ref, vtl_ref, o_ref, ot_ref, n_lat_tiles=SEQ // KT)
    for w_ref, w_out_ref in zip(w_refs, w_out_refs):
        w_out_ref[...] = w_ref[...].astype(BF16)


def _attn_ctx_kernel(qt_ref, kc_ref, vtc_ref, prev_ref, o_ref, ot_ref):
    del prev_ref
    _attn_body(qt_ref, kc_ref, vtc_ref, None, None, o_ref, ot_ref, n_lat_tiles=0)


def _attn_scratch():
    return [pltpu.VMEM((ATT_W, TQ), F32)]


def _attn_lat(qt, k, vt, casts):
    tq = TQ * Q_PER_STEP
    nq = SEQ // tq
    n_steps = BATCH * nq
    ctx0 = N_LAT // CTX_LEN
    cast_in, cast_out, cast_shape = [], [], []
    for w, layer in casts:
        n_rows, n_cols = w.shape[1:]
        blk = (n_rows // n_steps, n_cols)
        cast_in.append(pl.BlockSpec((None,) + blk, lambda b, j, layer=layer: (layer, b * nq + j, 0)))
        cast_out.append(pl.BlockSpec(blk, lambda b, j: (b * nq + j, 0)))
        cast_shape.append(jax.ShapeDtypeStruct((n_rows, n_cols), BF16))
    return pl.pallas_call(
        functools.partial(_attn_lat_kernel, n_cast=len(casts)),
        grid=(BATCH, nq),
        in_specs=[pl.BlockSpec((ATT_W, tq), lambda b, j: (0, b * nq + j)),
                  pl.BlockSpec((CTX_LEN, KV_W), lambda b, j: (ctx0 + b, 0)),
                  pl.BlockSpec((CTX_LEN // KT, KV_W, KT), lambda b, j: (ctx0 + b, 0, 0)),
                  pl.BlockSpec((SEQ, KV_W), lambda b, j: (b, 0)),
                  pl.BlockSpec((SEQ // KT, KV_W, KT), lambda b, j: (b, 0, 0))] + cast_in,
        out_specs=[pl.BlockSpec((tq, ATT_W), lambda b, j: (b * nq + j, 0))] + cast_out,
        out_shape=[jax.ShapeDtypeStruct((N_TOK, ATT_W), BF16)] + cast_shape,
        scratch_shapes=_attn_scratch(),
        compiler_params=_params(2),
        name="attn_latent",
    )(qt, k, vt, k, vt, *[w for w, _ in casts])


def _attn_ctx(qt, k, vt, att):
    ctx0 = N_LAT // CTX_LEN
    return pl.pallas_call(
        _attn_ctx_kernel,
        grid=(BATCH,),
        in_specs=[pl.BlockSpec((ATT_W, TQ), lambda b: (0, ctx0 + b)),
                  pl.BlockSpec((CTX_LEN, KV_W), lambda b: (ctx0 + b, 0)),
                  pl.BlockSpec((CTX_LEN // KT, KV_W, KT), lambda b: (ctx0 + b, 0, 0)),
                  pl.BlockSpec(memory_space=pl.ANY)],
        out_specs=pl.BlockSpec((TQ, ATT_W), lambda b: (ctx0 + b, 0)),
        out_shape=jax.ShapeDtypeStruct((N_TOK, ATT_W), BF16),
        input_output_aliases={3: 0},
        scratch_shapes=_attn_scratch(),
        compiler_params=_params(1),
        name="attn_context",
    )(qt, k, vt, att)


def _four_lat_kernel(zre0_ref, zre1_ref, zim0_ref, zim1_ref, cs_ref, tc_ref, ts_ref, o_ref,
                     are0_ref, are1_ref, aim0_ref, aim1_ref):
    rot = cs_ref[...]
    both = lambda lo_ref, hi_ref, idx: jnp.concatenate([lo_ref[idx, :], hi_ref[idx, :]], axis=1)
    for c in range(DFT_R):
        strided = pl.ds(c, DFT_R, stride=Z_PITCH)
        z = jnp.concatenate([both(zre0_ref, zre1_ref, strided), both(zim0_ref, zim1_ref, strided)], axis=0)
        a = jnp.dot(rot, z.astype(BF16), preferred_element_type=F32)
        are, aim = a[:DFT_R], a[DFT_R:]
        are0_ref[strided, :] = are[:, :LANES]
        are1_ref[strided, :] = are[:, LANES:]
        aim0_ref[strided, :] = aim[:, :LANES]
        aim1_ref[strided, :] = aim[:, LANES:]

    def group(lo_ref, hi_ref, m):
        parts = [both(lo_ref, hi_ref, pl.ds((m * F2_GROUP + g) * Z_PITCH, DFT_R)) for g in range(F2_GROUP)]
        return jnp.concatenate(parts, axis=0).astype(BF16)

    for m in range(DFT_R // F2_GROUP):
        y = (jnp.dot(tc_ref[m], group(are0_ref, are1_ref, m), preferred_element_type=F32)
             + jnp.dot(ts_ref[m], group(aim0_ref, aim1_ref, m), preferred_element_type=F32))
        o_ref[:, m, :, :] = y.reshape(DFT_R, F2_GROUP, FOUR_W)


def _four_lat(zre, zim, cs64, tc, ts):
    n_groups = DFT_R // F2_GROUP
    rows = F2_GROUP * DFT_R
    seq_rows = DFT_R * Z_PITCH
    lo = pl.BlockSpec((seq_rows, LANES), lambda b: (b, 0))
    hi = pl.BlockSpec((seq_rows, LANES), lambda b: (b, 1))
    t_spec = pl.BlockSpec((n_groups, rows, rows), lambda b: (0, 0, 0), pipeline_mode=pl.Buffered(1))
    yf = pl.pallas_call(
        _four_lat_kernel,
        grid=(BATCH,),
        in_specs=[lo, hi, lo, hi, _full((2 * DFT_R, 2 * DFT_R)), t_spec, t_spec],
        out_specs=pl.BlockSpec((DFT_R, n_groups, F2_GROUP, FOUR_W), lambda b: (b, 0, 0, 0)),
        out_shape=jax.ShapeDtypeStruct((N_TOK // DFT_R, n_groups, F2_GROUP, FOUR_W), F32),
        scratch_shapes=[pltpu.VMEM((seq_rows, LANES), F32)] * 4,
        compiler_params=_params(1),
        name="fourier_latent",
    )(zre, zre, zim, zim, cs64, tc, ts)
    return yf.reshape(N_TOK, FOUR_W)


def _four_ctx_kernel(zre_ref, zim_ref, c_ref, s_ref, prev_ref, o_ref):
    del prev_ref
    rows = lambda ref: jnp.concatenate(
        [ref[pl.ds(r * Z_PITCH, DFT_R), :] for r in range(CTX_LEN // DFT_R)], axis=0).astype(BF16)
    o_ref[...] = (jnp.dot(c_ref[...], rows(zre_ref), preferred_element_type=F32)
                  + jnp.dot(s_ref[...], rows(zim_ref), preferred_element_type=F32))


def _four_ctx(zre, zim, c256, s256, yf):
    ctx0 = N_LAT // CTX_LEN
    z_spec = pl.BlockSpec((CTX_LEN // DFT_R * Z_PITCH, FOUR_W), lambda b: (ctx0 + b, 0))
    blk = pl.BlockSpec((CTX_LEN, FOUR_W), lambda b: (ctx0 + b, 0))
    return pl.pallas_call(
        _four_ctx_kernel,
        grid=(BATCH,),
        in_specs=[z_spec, z_spec, _full((CTX_LEN, CTX_LEN)), _full((CTX_LEN, CTX_LEN)),
                  pl.BlockSpec(memory_space=pl.ANY)],
        out_specs=blk,
        out_shape=jax.ShapeDtypeStruct((N_TOK, FOUR_W), F32),
        input_output_aliases={4: 0},
        compiler_params=_params(1),
        name="fourier_context",
    )(zre, zim, c256, s256, yf)


def _outproj_kernel(*refs, n_x):
    ubc_ref, hp_ref, hn_ref, yf_ref, att_ref = refs[:5]
    mod_ref, g_ref, cw_ref, wo_ref, o_ref = refs[5 + n_x:]
    read_x = _token_rows(refs[5:5 + n_x])
    i = pl.program_id(0)
    ubc = ubc_ref[...].astype(F32)
    u, bg, cg = ubc[:, :CONV_W], ubc[:, CONV_W:2 * CONV_W], ubc[:, 2 * CONV_W:]
    z = cg * u
    hp = hp_ref[HALO - 1:HALO, :].astype(F32)
    hn = hn_ref[0:1, :].astype(F32)
    z_before = hp[:, 2 * CONV_W:] * hp[:, :CONV_W]
    z_after = hn[:, 2 * CONV_W:] * hn[:, :CONV_W]
    row = lax.broadcasted_iota(jnp.int32, (SUB, 1), 0)
    w_mix = jnp.concatenate([wo_ref[CONV_W + FOUR_W:, :], wo_ref[CONV_W:CONV_W + FOUR_W, :],
                             wo_ref[:CONV_W, :]], axis=0)
    gate_gain = mod_ref[2:3, :] * g_ref[...]
    for sb in range(TM // SUB):
        rows = slice(sb * SUB, (sb + 1) * SUB)
        tok = i * TM + sb * SUB
        seg = jnp.where(tok < N_LAT, SEQ, CTX_LEN)
        starts = jnp.bitwise_and(tok, seg - 1) == 0
        ends = jnp.bitwise_and(tok + SUB, seg - 1) == 0
        above = z_before if sb == 0 else z[sb * SUB - 1:sb * SUB]
        below = z_after if (sb + 1) * SUB == TM else z[(sb + 1) * SUB:(sb + 1) * SUB + 1]
        above = jnp.where(starts, 0.0, above)
        below = jnp.where(ends, 0.0, below)
        zs = z[rows]
        z_prev = jnp.where(row == 0, above, pltpu.roll(zs, 1, 0))
        z_next = jnp.where(row == SUB - 1, below, pltpu.roll(zs, SUB - 1, 0))
        conv = bg[rows] * (z_prev * cw_ref[0:1, :] + zs * cw_ref[1:2, :] + z_next * cw_ref[2:3, :])
        mix = jnp.concatenate([att_ref[rows, :], yf_ref[rows, :].astype(BF16), conv.astype(BF16)], axis=1)
        y = jnp.dot(mix, w_mix, preferred_element_type=F32)
        scale = lax.rsqrt(jnp.mean(y * y, axis=-1, keepdims=True) + EPS)
        o_ref[rows, :] = read_x(rows) + y * scale * gate_gain


def _outproj(l, n_rows, ubc, yf, att, xs, mod, g, cw, wo):
    n_tiles = n_rows // TM
    per = TM // HALO
    last_halo = n_rows // HALO - 1
    row = lambda i: (i, 0)
    return pl.pallas_call(
        functools.partial(_outproj_kernel, n_x=len(xs)),
        grid=(n_tiles,),
        in_specs=[pl.BlockSpec((TM, 3 * CONV_W), row),
                  pl.BlockSpec((HALO, 3 * CONV_W), lambda i: (jnp.maximum(i * per - 1, 0), 0)),
                  pl.BlockSpec((HALO, 3 * CONV_W), lambda i: (jnp.minimum((i + 1) * per, last_halo), 0)),
                  pl.BlockSpec((TM, FOUR_W), row),
                  pl.BlockSpec((TM, ATT_W), row)] + _token_specs(len(xs) == 2) + [
                  _mod_spec(l, TM),
                  _layer(l, (1, D_MODEL)),
                  _layer(l, (3, CONV_W)),
                  _full((MIX_W_ROWS, D_MODEL))],
        out_specs=pl.BlockSpec((TM, D_MODEL), row),
        out_shape=jax.ShapeDtypeStruct((n_rows, D_MODEL), F32),
        compiler_params=_params(1),
        name="out_proj",
    )(ubc, ubc, ubc, yf, att, *xs, mod, g, cw, wo)


FF_CHUNK = 1024
FF_LEAD = 1


def _ffn_kernel(x_ref, mod_ref, gpre_ref, gpost_ref, wu_ref, wd_ref, o_ref):
    n_chunks = D_FF // FF_CHUNK
    work = [(sb, c) for sb in range(TM // SUB) for c in range(n_chunks)]

    normed = {}

    def up(sb, c):
        if sb not in normed:
            x = x_ref[sb * SUB:(sb + 1) * SUB, :]
            normed[sb] = (_rms(x, gpre_ref[...] * (1.0 + mod_ref[4:5, :])) + mod_ref[3:4, :]).astype(BF16)
        return jnp.dot(normed[sb], wu_ref[:, c * FF_CHUNK:(c + 1) * FF_CHUNK], preferred_element_type=F32)

    def down(c, a):
        a = jnp.maximum(a, 0.0)
        return jnp.dot((a * a).astype(BF16), wd_ref[c * FF_CHUNK:(c + 1) * FF_CHUNK, :],
                       preferred_element_type=F32)

    pending = {}
    acc = None
    for step in range(len(work) + FF_LEAD):
        if step < len(work):
            pending[step] = up(*work[step])
        if step >= FF_LEAD:
            sb, c = work[step - FF_LEAD]
            d = down(c, pending.pop(step - FF_LEAD))
            acc = d if c == 0 else acc + d
            if c == n_chunks - 1:
                rows = slice(sb * SUB, (sb + 1) * SUB)
                o_ref[rows, :] = x_ref[rows, :] + _rms(acc, mod_ref[5:6, :] * gpost_ref[...])


def _ffn(l, n_rows, x, mod, gpre, gpost, wu, wd):
    row = lambda i: (i, 0)
    return pl.pallas_call(
        _ffn_kernel,
        grid=(n_rows // TM,),
        in_specs=[pl.BlockSpec((TM, D_MODEL), row),
                  _mod_spec(l, TM),
                  _layer(l, (1, D_MODEL)),
                  _layer(l, (1, D_MODEL)),
                  pl.BlockSpec((D_MODEL, D_FF), lambda i: (0, 0), pipeline_mode=pl.Buffered(1)),
                  pl.BlockSpec((D_FF, D_MODEL), lambda i: (0, 0), pipeline_mode=pl.Buffered(1))],
        out_specs=pl.BlockSpec((TM, D_MODEL), row),
        out_shape=jax.ShapeDtypeStruct((n_rows, D_MODEL), F32),
        compiler_params=_params(1),
        name="ffn",
    )(x, mod, gpre, gpost, wu, wd)


def _rope_tables():
    t = jnp.arange(SEQ)
    rowp = (t // GRID_W).astype(F32)
    colp = (t % GRID_W).astype(F32)
    inv = 1.0 / (ROPE_THETA ** (jnp.arange(0, AXIS_DIM, 2, dtype=F32) / AXIS_DIM))
    ang = jnp.stack([rowp[:, None] * inv, colp[:, None] * inv], axis=1)
    ang = jnp.broadcast_to(ang[:, :, None, :], (SEQ, 2, 2, AXIS_DIM // 2)).reshape(SEQ, HEAD_DIM)
    cos, sin = jnp.cos(ang), jnp.sin(ang)
    first_half = (jnp.arange(HEAD_DIM) % AXIS_DIM) < AXIS_DIM // 2
    sin_lo = jnp.where(first_half, -sin, 0.0)
    sin_hi = jnp.where(first_half, 0.0, sin)
    ident = lambda v: jnp.full((TM, HEAD_DIM), v, F32)
    wide = lambda a, v: jnp.tile(jnp.concatenate([a, ident(v)], axis=0), (1, LANES // HEAD_DIM))
    return wide(cos, 1.0), wide(sin_lo, 0.0), wide(sin_hi, 0.0)


def _dft_tables():
    two_pi = 2.0 * np.pi
    n = np.arange(FOUR_GROUP_DIM)
    ang = two_pi * np.outer(n, n) / FOUR_GROUP_DIM
    eye = np.eye(FOUR_GROUPS)
    scale = FOUR_GROUP_DIM ** -0.5
    mch = np.concatenate([np.kron(eye, np.cos(ang)), -np.kron(eye, np.sin(ang))], axis=1) * scale
    r = np.arange(DFT_R)
    ang_r = two_pi * np.outer(r, r) / DFT_R
    cs64 = np.block([[np.cos(ang_r), np.sin(ang_r)], [-np.sin(ang_r), np.cos(ang_r)]])
    n_groups = DFT_R // F2_GROUP
    m = np.arange(n_groups)[:, None, None, None]
    k2 = np.arange(DFT_R)[None, :, None, None]
    g = np.arange(F2_GROUP)[None, None, :, None]
    c = np.arange(DFT_R)[None, None, None, :]
    k = F2_GROUP * m + g + DFT_R * k2
    ang2 = two_pi * ((k * c) % SEQ) / SEQ
    sel = np.eye(F2_GROUP)[None, None, :, :, None]
    norm2 = SEQ ** -0.5
    tc = (np.cos(ang2)[:, :, :, None, :] * sel * norm2).reshape(n_groups, DFT_R * F2_GROUP, F2_GROUP * DFT_R)
    ts = (np.sin(ang2)[:, :, :, None, :] * sel * norm2).reshape(n_groups, DFT_R * F2_GROUP, F2_GROUP * DFT_R)
    t = np.arange(CTX_LEN)
    ang_c = two_pi * (np.outer(t, t) % CTX_LEN) / CTX_LEN
    normc = CTX_LEN ** -0.5
    c256, s256 = np.cos(ang_c) * normc, np.sin(ang_c) * normc
    as_bf16 = lambda a: jnp.asarray(a, dtype=F32).astype(BF16)
    return tuple(as_bf16(a) for a in (mch, cs64, tc, ts, c256, s256))


def _head_mean_matrix():
    blocks = np.kron(np.eye(N_HEADS), np.full((HEAD_DIM, HEAD_DIM), 1.0 / HEAD_DIM))
    return jnp.asarray(blocks, dtype=F32).astype(BF16)


def kernel(x, c, ctx, c_ctx, w_ada, b_ada, w_in, conv_w, q_gain, k_gain, w_out,
           g_pre_mix, g_post_mix, g_pre_ffn, g_post_ffn, w_ffn_up, w_ffn_down):
    cos, sin_lo, sin_hi = _rope_tables()
    mch, cs64, tc, ts, c256, s256 = _dft_tables()
    ones = _head_mean_matrix()

    c8 = jnp.concatenate([c, c_ctx[None, :], jnp.zeros((MOD_ROWS - BATCH - 1, D_MODEL), F32)], axis=0)
    mod = _ada(c8, w_ada, b_ada).reshape(DEPTH, MOD_ROWS, N_MOD, D_MODEL)

    xs = (x.reshape(N_LAT, D_MODEL), ctx.reshape(N_CTX, D_MODEL))
    w_in_b = w_in[0].astype(BF16)
    rows = lambda a: a.reshape(DEPTH, 1, -1)
    qg, kg = rows(jnp.tile(q_gain, (1, N_HEADS))), rows(jnp.tile(k_gain, (1, N_KV)))
    g_pre_mix, g_post_mix, g_pre_ffn, g_post_ffn = map(rows, (g_pre_mix, g_post_mix, g_pre_ffn, g_post_ffn))

    for l in range(DEPTH):
        last = l == DEPTH - 1
        ubc, zre, zim, qt, k, vt = _inproj(l, xs, mod, g_pre_mix, w_in_b, mch, ones,
                                           qg, kg, cos, sin_lo, sin_hi)
        casts = [(w_out, l), (w_ffn_up, l), (w_ffn_down, l)] + ([] if last else [(w_in, l + 1)])
        att, w_out_b, w_up_b, w_down_b, *w_in_next = _attn_lat(qt, k, vt, casts)
        if not last:
            w_in_b = w_in_next[0]
        yf = _four_lat(zre, zim, cs64, tc, ts)
        if last:
            n_rows = N_LAT
        else:
            n_rows = N_TOK
            att = _attn_ctx(qt, k, vt, att)
            yf = _four_ctx(zre, zim, c256, s256, yf)
        x_mid = _outproj(l, n_rows, ubc, yf, att, xs, mod, g_post_mix, conv_w, w_out_b)
        xs = (_ffn(l, n_rows, x_mid, mod, g_pre_ffn, g_post_ffn, w_up_b, w_down_b),)
    return xs[0].reshape(BATCH, SEQ, D_MODEL)
```
